```python
import math
import jax, jax.numpy as jnp
from jax import lax
import numpy as np

D_MODEL = 1024
BATCH = 8
SEQ = 4096
DEPTH = 4

MEM_LEN = 256
EPS = 1e-6
DA_HEADS = 4
DA_HEAD_DIM = 64
DA_WIDTH = DA_HEADS * 2 * DA_HEAD_DIM
ROT_DIM = DA_HEAD_DIM // 4
ROPE_THETA = 500000.0
Q_BLOCK = 128
POOL_WINDOWS = (2, 4, 8, 16)
POOL_GROUPS = 4
POOL_GROUP_CH = 128
POOL_WIDTH = POOL_GROUPS * POOL_GROUP_CH
CHUNK = 128
SG_GROUPS = 4
SG_WIDTH = 512
SG_GROUP_CH = SG_WIDTH // SG_GROUPS
N_BRANCH = 3
BRANCH_WIDTH = 512
X_HEADS = 4
X_HEAD_DIM = D_MODEL // X_HEADS
PEER_HEADS = 8
N_KEYS = 128
N_EXPERTS = N_KEYS * N_KEYS
PEER_TOPK = 16
PEER_QDIM = 256
PEER_HALF = PEER_QDIM // 2
PEER_BLOCK = 128
IN_WIDTH = 3 * DA_WIDTH + POOL_WIDTH + 2 * SG_WIDTH + N_BRANCH * D_MODEL

kernel_name = "hybrid_diffattn_pool_sgmlp_peer"


def rmsnorm(x, g):
    xf = x.astype(jnp.float32)
    y = xf * lax.rsqrt(jnp.mean(xf * xf, axis=-1, keepdims=True) + EPS)
    return (y * g.astype(jnp.float32)).astype(x.dtype)


def layernorm(x, g, b):
    xf = x.astype(jnp.float32)
    mu = jnp.mean(xf, axis=-1, keepdims=True)
    var = jnp.mean(jnp.square(xf - mu), axis=-1, keepdims=True)
    y = (xf - mu) * lax.rsqrt(var + EPS) * g.astype(jnp.float32) + b.astype(jnp.float32)
    return y.astype(x.dtype)


def rope_tables(positions):
    inv = ROPE_THETA ** (-jnp.arange(0, ROT_DIM, 2, dtype=jnp.float32) / ROT_DIM)
    ang = positions.astype(jnp.float32)[..., None] * inv
    return jnp.cos(ang), jnp.sin(ang)


def apply_partial_rope(x, cos, sin):
    half = ROT_DIM // 2
    xf = x[..., :ROT_DIM].astype(jnp.float32)
    x1, x2 = xf[..., :half], xf[..., half:]
    c, s = cos[:, :, None, :], sin[:, :, None, :]
    rot = jnp.concatenate([x1 * c - x2 * s, x1 * s + x2 * c], axis=-1).astype(x.dtype)
    return jnp.concatenate([rot, x[..., ROT_DIM:]], axis=-1)


def diff_attention(q, k, v, cos, sin, lam_qk, subln_g, lam_init):
    B, S, _ = q.shape
    H, DH = DA_HEADS, DA_HEAD_DIM
    q = apply_partial_rope(q.reshape(B, S, H * 2, DH), cos, sin).reshape(B, S, H, 2, DH)
    k = apply_partial_rope(k.reshape(B, S, H * 2, DH), cos, sin).reshape(B, S, H, 2, DH)
    v = v.reshape(B, S, H, 2 * DH)
    lq = lam_qk.astype(jnp.float32)
    lam = jnp.exp(jnp.sum(lq[0] * lq[1])) - jnp.exp(jnp.sum(lq[2] * lq[3])) + lam_init
    nb = S // Q_BLOCK
    qb = q.reshape(B, nb, Q_BLOCK, H, 2, DH).transpose(1, 0, 3, 4, 2, 5)
    kt = k.transpose(0, 2, 3, 1, 4)
    vt = v.transpose(0, 2, 1, 3)
    key_pos = jnp.arange(S)
    scale = DH ** -0.5

    def block(args):
        qblk, start = args
        s = jnp.einsum('bhmqd,bhmkd->bhmqk', qblk, kt).astype(jnp.float32) * scale
        causal = key_pos[None, :] <= (start + jnp.arange(Q_BLOCK))[:, None]
        p = jax.nn.softmax(jnp.where(causal, s, -jnp.inf), axis=-1)
        a = p[:, :, 0] - lam * p[:, :, 1]
        return jnp.einsum('bhqk,bhkd->bhqd', a.astype(vt.dtype), vt)

    o = lax.map(block, (qb, jnp.arange(nb) * Q_BLOCK))
    o = o.transpose(1, 0, 3, 2, 4).reshape(B, S, H, 2 * DH)
    o = rmsnorm(o, subln_g) * (1.0 - lam_init)
    return o.reshape(B, S, H * 2 * DH)


def multiscale_pool(p, w_pool, pool_scale):
    B, S, _ = p.shape
    pg = p.reshape(B, S, POOL_GROUPS, POOL_GROUP_CH).astype(jnp.float32)
    cs = jnp.concatenate([jnp.zeros_like(pg[:, :1]), lax.cumsum(pg, axis=1)], axis=1)
    t = jnp.arange(S)
    groups = []
    for g, w in enumerate(POOL_WINDOWS):
        cg = cs[:, :, g]
        lo = jnp.maximum(t + 1 - w, 0)
        cnt = jnp.minimum(t + 1, w).astype(jnp.float32)
        mean = (cg[:, 1:] - cg[:, lo]) / cnt[None, :, None]
        groups.append(mean - pg[:, :, g])
    d = jnp.stack(groups, axis=2).astype(p.dtype)
    y = jnp.einsum('bsgc,gce->bsge', d, w_pool)
    return y.reshape(B, S, POOL_WIDTH) * pool_scale


def chunk_spatial_gate(u, v, ln_g, ln_b, w_s, b_s):
    B, S, _ = v.shape
    vn = layernorm(v, ln_g, ln_b).reshape(B, S // CHUNK, CHUNK, SG_GROUPS, SG_GROUP_CH)
    ws = w_s * jnp.tril(jnp.ones((CHUNK, CHUNK), w_s.dtype))
    mixed = jnp.einsum('gts,bnsgc->bntgc', ws, vn) + b_s.T[None, None, :, :, None]
    return u * mixed.reshape(B, S, SG_WIDTH)


def memory_cross_attention(h, m, w_q, w_kv, w_o):
    B, S, _ = h.shape
    M = m.shape[1]
    q = (h @ w_q).reshape(B, S, X_HEADS, X_HEAD_DIM)
    kv = (m @ w_kv).reshape(B, M, 2, X_HEADS, X_HEAD_DIM)
    s = jnp.einsum('bshd,bmhd->bhsm', q, kv[:, :, 0]).astype(jnp.float32) * (X_HEAD_DIM ** -0.5)
    p = jax.nn.softmax(s, axis=-1).astype(h.dtype)
    o = jnp.einsum('bhsm,bmhd->bshd', p, kv[:, :, 1])
    return o.reshape(B, S, D_MODEL) @ w_o


def peer_ffn(x, w_pq, sub_keys, peer_u, peer_v):
    B, S, D = x.shape
    q = jnp.einsum('bsd,dhk->bshk', x, w_pq).reshape(B, S, PEER_HEADS, 2, PEER_HALF)
    sc = jnp.einsum('bshpk,hpnk->bshpn', q, sub_keys).astype(jnp.float32)
    s_top, i_top = lax.top_k(sc, PEER_TOPK)
    cand = (s_top[..., 0, :, None] + s_top[..., 1, None, :]).reshape(B, S, PEER_HEADS, PEER_TOPK * PEER_TOPK)
    cand_idx = (i_top[..., 0, :, None] * N_KEYS + i_top[..., 1, None, :]).reshape(B, S, PEER_HEADS, PEER_TOPK * PEER_TOPK)
    f_s, f_pos = lax.top_k(cand, PEER_TOPK)
    e_idx = jnp.take_along_axis(cand_idx, f_pos, axis=-1)
    gates = jax.nn.softmax(f_s, axis=-1).astype(x.dtype)
    T = B * S
    nblk = T // PEER_BLOCK
    xt = x.reshape(nblk, PEER_BLOCK, D)
    it = e_idx.reshape(nblk, PEER_BLOCK, PEER_HEADS * PEER_TOPK)
    gt = gates.reshape(nblk, PEER_BLOCK, PEER_HEADS * PEER_TOPK)

    def blk(args):
        xb, ib, gb = args
        ub = peer_u[ib]
        a = jax.nn.gelu(jnp.einsum('td,ted->te', xb, ub), approximate=False)
        vb = peer_v[ib]
        return jnp.einsum('te,ted->td', a * gb, vb)

    return lax.map(blk, (xt, it, gt)).reshape(B, S, D)


def setup_inputs(seed: int = 0) -> dict:
    key = jax.random.key(seed)
    ks = jax.random.split(key, 32)
    L, D = DEPTH, D_MODEL

    def nrm(k, shape, scale):
        return jax.random.normal(k, shape, jnp.float32) * scale

    def gain(k, shape):
        return 1.0 + 0.05 * jax.random.normal(k, shape, jnp.float32)

    start = jax.random.randint(ks[2], (BATCH, 1), 0, 1024, dtype=jnp.int32)
    positions = (start + jnp.arange(SEQ, dtype=jnp.int32)[None, :]).astype(jnp.int32)
    return {
        'x': nrm(ks[0], (BATCH, SEQ, D), 1.0),
        'mem': nrm(ks[1], (BATCH, MEM_LEN, D), 1.0),
        'positions': positions,
        'g_mix': gain(ks[3], (L, D)),
        'w_in': nrm(ks[4], (L, D, IN_WIDTH), D ** -0.5),
        'lam_qk': nrm(ks[5], (L, 4, DA_HEAD_DIM), 0.1),
        'subln_g': gain(ks[6], (L, 2 * DA_HEAD_DIM)),
        'w_pool': nrm(ks[7], (L, POOL_GROUPS, POOL_GROUP_CH, POOL_GROUP_CH), POOL_GROUP_CH ** -0.5),
        'pool_scale': gain(ks[8], (L, POOL_WIDTH)),
        'sg_ln_g': gain(ks[9], (L, SG_WIDTH)),
        'sg_ln_b': nrm(ks[10], (L, SG_WIDTH), 0.02),
        'w_spatial': nrm(ks[11], (L, SG_GROUPS, CHUNK, CHUNK), 0.5 * CHUNK ** -0.5),
        'b_spatial': gain(ks[12], (L, SG_GROUPS, CHUNK)),
        'w_branch': nrm(ks[13], (L, N_BRANCH, BRANCH_WIDTH, D), BRANCH_WIDTH ** -0.5),
        'w_out': nrm(ks[14], (L, D, D), D ** -0.5),
        'g_cross': gain(ks[15], (L, D)),
        'g_mem': gain(ks[16], (L, D)),
        'w_xq': nrm(ks[17], (L, D, D), D ** -0.5),
        'w_xkv': nrm(ks[18], (L, D, 2 * D), D ** -0.5),
        'w_xo': nrm(ks[19], (L, D, D), D ** -0.5),
        'g_ffn': gain(ks[20], (L, D)),
        'w_pq': nrm(ks[21], (L, D, PEER_HEADS, PEER_QDIM), D ** -0.5),
        'sub_keys': nrm(ks[22], (L, PEER_HEADS, 2, N_KEYS, PEER_HALF), PEER_HALF ** -0.5),
        'peer_u': nrm(ks[23], (L, N_EXPERTS, D), D ** -0.5),
        'peer_v': nrm(ks[24], (L, N_EXPERTS, D), 0.2),
        'g_final': gain(ks[25], (D,)),
    }


def reference(x, mem, positions, g_mix, w_in, lam_qk, subln_g, w_pool, pool_scale, sg_ln_g, sg_ln_b, w_spatial, b_spatial, w_branch, w_out, g_cross, g_mem, w_xq, w_xkv, w_xo, g_ffn, w_pq, sub_keys, peer_u, peer_v, g_final):
    B, S, D = x.shape
    cos, sin = rope_tables(positions)
    sizes = (DA_WIDTH, DA_WIDTH, DA_WIDTH, POOL_WIDTH, SG_WIDTH, SG_WIDTH)
    pts = []
    acc = 0
    for sz in sizes:
        acc += sz
        pts.append(acc)
    for l in range(DEPTH):
        lam_init = 0.8 - 0.6 * math.exp(-0.3 * l)
        xn = rmsnorm(x, g_mix[l])
        z = xn @ w_in[l]
        q_a, k_a, v_a, p_in, u_in, v_in, gate_in = jnp.split(z, pts, axis=-1)
        a_out = diff_attention(q_a, k_a, v_a, cos, sin, lam_qk[l], subln_g[l], lam_init)
        p_out = multiscale_pool(p_in, w_pool[l], pool_scale[l])
        s_out = chunk_spatial_gate(u_in, v_in, sg_ln_g[l], sg_ln_b[l], w_spatial[l], b_spatial[l])
        branches = jnp.stack([a_out, p_out, s_out], axis=2)
        proj = jnp.einsum('bsnc,ncd->bsnd', branches, w_branch[l])
        gates = jax.nn.sigmoid(gate_in.reshape(B, S, N_BRANCH, D))
        merged = jnp.sum(gates * proj, axis=2)
        x = x + merged @ w_out[l]
        x = x + memory_cross_attention(rmsnorm(x, g_cross[l]), rmsnorm(mem, g_mem[l]), w_xq[l], w_xkv[l], w_xo[l])
        x = x + peer_ffn(rmsnorm(x, g_ffn[l]), w_pq[l], sub_keys[l], peer_u[l], peer_v[l])
    return rmsnorm(x, g_final)
```

```python
import functools
import math

import jax
import jax.numpy as jnp
from jax import lax
from jax.experimental import pallas as pl
from jax.experimental.pallas import tpu as pltpu

F32 = jnp.float32
BF16 = jnp.bfloat16

EPS = 1e-6
LANES = 128
VMEM_LIMIT_BYTES = 48 * 1024 * 1024

DA_HEADS = 4
DA_HEAD_DIM = 64
ROT_DIM = DA_HEAD_DIM // 4
ROPE_THETA = 500000.0
POOL_WINDOWS = (2, 4, 8, 16)
CHUNK = 128
SG_GROUPS = 4
N_BRANCH = 3
X_HEADS = 4
PEER_HEADS = 8
N_KEYS = 128
PEER_TOPK = 16
PEER_CELLS = tuple((a, b) for a in range(PEER_TOPK) for b in range(PEER_TOPK) if (a + 1) * (b + 1) <= PEER_TOPK)


def _params(*sem):
    return pltpu.CompilerParams(dimension_semantics=sem, vmem_limit_bytes=VMEM_LIMIT_BYTES)


def _rmsnorm_rows(x, g):
    ms = jnp.mean(x * x, axis=-1, keepdims=True)
    return x * lax.rsqrt(ms + EPS) * g


def _norm_matmul_kernel(x_ref, g_ref, w_ref, o_ref, xn_ref):
    @pl.when(pl.program_id(1) == 0)
    def _():
        xn_ref[...] = _rmsnorm_rows(x_ref[...], g_ref[...]).astype(BF16)

    o_ref[...] = jnp.dot(xn_ref[...], w_ref[...], preferred_element_type=F32).astype(o_ref.dtype)


def norm_matmul(x, g, w, *, tm, tn, name):
    T, D = x.shape
    N = w.shape[1]
    return pl.pallas_call(
        _norm_matmul_kernel,
        grid=(T // tm, N // tn),
        in_specs=[pl.BlockSpec((tm, D), lambda i, j: (i, 0)),
                  pl.BlockSpec((1, D), lambda i, j: (0, 0)),
                  pl.BlockSpec((D, tn), lambda i, j: (0, j))],
        out_specs=pl.BlockSpec((tm, tn), lambda i, j: (i, j)),
        out_shape=jax.ShapeDtypeStruct((T, N), BF16),
        scratch_shapes=[pltpu.VMEM((tm, D), BF16)],
        compiler_params=_params("parallel", "arbitrary"),
        name=name,
    )(x, g.reshape(1, D), w)


def _rope_kernel(z_ref, c_ref, s_ref, p_ref, o_ref, *, n_q_groups):
    c = c_ref[...]
    s = s_ref[...]
    for gi in range(z_ref.shape[1] // LANES):
        cols = slice(gi * LANES, (gi + 1) * LANES)
        xg = z_ref[:, cols]
        swapped = jnp.dot(xg, p_ref[...], preferred_element_type=F32)
        r = xg.astype(F32) * c + swapped * s
        if gi < n_q_groups:
            r = r * (DA_HEAD_DIM ** -0.5)
        o_ref[:, cols] = r.astype(o_ref.dtype)


def rope_qk(z, cos_tab, sin_tab, perm, *, tm):
    T = z.shape[0]
    width = 2 * DA_HEADS * 2 * DA_HEAD_DIM
    return pl.pallas_call(
        functools.partial(_rope_kernel, n_q_groups=width // (2 * LANES)),
        grid=(T // tm,),
        in_specs=[pl.BlockSpec((tm, width), lambda i: (i, 0)),
                  pl.BlockSpec((tm, LANES), lambda i: (i, 0)),
                  pl.BlockSpec((tm, LANES), lambda i: (i, 0)),
                  pl.BlockSpec((LANES, LANES), lambda i: (0, 0))],
        out_specs=pl.BlockSpec((tm, width), lambda i: (i, 0)),
        out_shape=jax.ShapeDtypeStruct((T, width), BF16),
        compiler_params=_params("parallel"),
        name="rope_qk",
    )(z, cos_tab, sin_tab, perm)


def rope_tables(positions):
    half = ROT_DIM // 2
    inv = ROPE_THETA ** (-jnp.arange(0, ROT_DIM, 2, dtype=F32) / ROT_DIM)
    ang = positions.astype(F32).reshape(-1, 1) * inv
    cos, sin = jnp.cos(ang), jnp.sin(ang)
    T = ang.shape[0]
    ones = jnp.ones((T, DA_HEAD_DIM - ROT_DIM), F32)
    zeros = jnp.zeros((T, DA_HEAD_DIM - ROT_DIM), F32)
    c64 = jnp.concatenate([cos, cos, ones], axis=1)
    s64 = jnp.concatenate([-sin, sin, zeros], axis=1)
    lane = jnp.arange(LANES)
    d = lane % DA_HEAD_DIM
    src = jnp.where(d < half, lane + half, jnp.where(d < ROT_DIM, lane - half, -1))
    perm = (lane[:, None] == src[None, :]).astype(BF16)
    return jnp.tile(c64, (1, 2)), jnp.tile(s64, (1, 2)), perm


def _diff_attn_kernel(lam_ref, q_ref, k_ref, v_ref, g_ref, o_ref, *, tq, lam_init):
    qi = pl.program_id(2)
    q = q_ref[...]
    lane = lax.broadcasted_iota(jnp.int32, q.shape, 1)
    zero = jnp.zeros_like(q)
    q_maps = (jnp.where(lane < DA_HEAD_DIM, q, zero), jnp.where(lane >= DA_HEAD_DIM, q, zero))
    row = lax.broadcasted_iota(jnp.int32, (tq, tq), 0)
    col = lax.broadcasted_iota(jnp.int32, (tq, tq), 1)

    def block(kb, carry, diagonal):
        start = pl.multiple_of(kb * tq, tq)
        k = k_ref[pl.ds(start, tq), :]
        v = v_ref[pl.ds(start, tq), :]
        new = []
        for mi in range(2):
            m, l, acc = carry[3 * mi:3 * mi + 3]
            s = lax.dot_general(q_maps[mi], k, (((1,), (1,)), ((), ())), preferred_element_type=F32)
            if diagonal:
                s = jnp.where(col <= row, s, -jnp.inf)
            m_new = jnp.maximum(m, jnp.max(s, axis=1, keepdims=True))
            p = jnp.exp(s - m_new)
            alpha = jnp.exp(m - m_new)
            l = alpha * l + jnp.sum(p, axis=1, keepdims=True)
            acc = alpha * acc + jnp.dot(p.astype(BF16), v, preferred_element_type=F32)
            new += [m_new, l, acc]
        return tuple(new)

    init = (jnp.full((tq, 1), -jnp.inf, F32), jnp.zeros((tq, 1), F32), jnp.zeros((tq, LANES), F32)) * 2
    carry = lax.fori_loop(0, qi, lambda kb, c: block(kb, c, False), init)
    m1, l1, a1, m2, l2, a2 = block(qi, carry, True)
    o = a1 / l1 - lam_ref[0] * (a2 / l2)
    o_ref[...] = (_rmsnorm_rows(o, g_ref[...]) * (1.0 - lam_init)).astype(o_ref.dtype)


def diff_attention(qk, z, lam, subln_g, *, batch, seq, tq, lam_init):
    T = qk.shape[0]
    nq = seq // tq
    width = DA_HEADS * 2 * DA_HEAD_DIM
    return pl.pallas_call(
        functools.partial(_diff_attn_kernel, tq=tq, lam_init=lam_init),
        grid=(batch, DA_HEADS, nq),
        in_specs=[pl.BlockSpec(memory_space=pltpu.SMEM),
                  pl.BlockSpec((tq, LANES), lambda b, h, i: (b * nq + i, h)),
                  pl.BlockSpec((seq, LANES), lambda b, h, i: (b, DA_HEADS + h)),
                  pl.BlockSpec((seq, LANES), lambda b, h, i: (b, 2 * DA_HEADS + h)),
                  pl.BlockSpec((1, LANES), lambda b, h, i: (0, 0))],
        out_specs=pl.BlockSpec((tq, LANES), lambda b, h, i: (b * nq + i, h)),
        out_shape=jax.ShapeDtypeStruct((T, width), BF16),
        compiler_params=_params("parallel", "parallel", "arbitrary"),
        name="diff_attention",
    )(lam, qk, qk, z, subln_g.reshape(1, LANES))


def _pool_kernel(p_ref, prev_ref, cur_band_ref, prev_band_ref, inv_ref, w_ref, scale_ref, o_ref):
    has_prev = pl.program_id(1) > 0
    for g in range(len(POOL_WINDOWS)):
        cols = slice(g * LANES, (g + 1) * LANES)
        pg = p_ref[:, cols]
        wsum = jnp.dot(cur_band_ref[g], pg, preferred_element_type=F32)
        halo = jnp.dot(prev_band_ref[g], prev_ref[:, cols], preferred_element_type=F32)
        wsum = wsum + jnp.where(has_prev, halo, 0.0)
        d = wsum * inv_ref[:, cols] - pg.astype(F32)
        y = jnp.dot(d.astype(BF16), w_ref[g], preferred_element_type=F32) * scale_ref[:, cols]
        o_ref[:, cols] = y.astype(o_ref.dtype)


def multiscale_pool(z, w_pool, pool_scale, *, batch, seq, tc):
    T = z.shape[0]
    nc = seq // tc
    width = len(POOL_WINDOWS) * LANES
    col_block = 3
    t = jnp.arange(tc)
    delta = t[:, None] - t[None, :]
    win = jnp.asarray(POOL_WINDOWS)[:, None, None]
    cur_band = ((delta >= 0) & (delta < win)).astype(BF16)
    prev_band = ((delta + tc >= 0) & (delta + tc < win)).astype(BF16)
    pos = jnp.arange(seq)
    inv_cnt = 1.0 / jnp.minimum(pos[:, None] + 1, jnp.repeat(jnp.asarray(POOL_WINDOWS), LANES)[None, :]).astype(F32)
    return pl.pallas_call(
        _pool_kernel,
        grid=(batch, nc),
        in_specs=[pl.BlockSpec((tc, width), lambda b, c: (b * nc + c, col_block)),
                  pl.BlockSpec((tc, width), lambda b, c: (b * nc + jnp.maximum(c - 1, 0), col_block)),
                  pl.BlockSpec((len(POOL_WINDOWS), tc, tc), lambda b, c: (0, 0, 0)),
                  pl.BlockSpec((len(POOL_WINDOWS), tc, tc), lambda b, c: (0, 0, 0)),
                  pl.BlockSpec((tc, width), lambda b, c: (c, 0)),
                  pl.BlockSpec((len(POOL_WINDOWS), LANES, LANES), lambda b, c: (0, 0, 0)),
                  pl.BlockSpec((1, width), lambda b, c: (0, 0))],
        out_specs=pl.BlockSpec((tc, width), lambda b, c: (b * nc + c, 0)),
        out_shape=jax.ShapeDtypeStruct((T, width), BF16),
        compiler_params=_params("parallel", "arbitrary"),
        name="multiscale_pool",
    )(z, z, cur_band, prev_band, inv_cnt, w_pool.astype(BF16), pool_scale.reshape(1, width))


def _sgate_kernel(u_ref, v_ref, g_ref, b_ref, ws_ref, bias_ref, o_ref):
    v = v_ref[...].astype(F32)
    mu = jnp.mean(v, axis=-1, keepdims=True)
    vc = v - mu
    var = jnp.mean(vc * vc, axis=-1, keepdims=True)
    vn = (vc * lax.rsqrt(var + EPS) * g_ref[...] + b_ref[...]).astype(BF16)
    for n in range(v.shape[0] // CHUNK):
        rows = slice(n * CHUNK, (n + 1) * CHUNK)
        for g in range(SG_GROUPS):
            cols = slice(g * LANES, (g + 1) * LANES)
            mixed = jnp.dot(ws_ref[g], vn[rows, cols], preferred_element_type=F32) + bias_ref[:, cols]
            o_ref[rows, cols] = (u_ref[rows, cols].astype(F32) * mixed).astype(o_ref.dtype)


def chunk_spatial_gate(z, ln_g, ln_b, w_s, b_s, *, tc):
    T = z.shape[0]
    width = SG_GROUPS * LANES
    ws = (w_s * jnp.tril(jnp.ones((CHUNK, CHUNK), w_s.dtype))).astype(BF16)
    bias = jnp.repeat(b_s.T, LANES, axis=1)
    return pl.pallas_call(
        _sgate_kernel,
        grid=(T // tc,),
        in_specs=[pl.BlockSpec((tc, width), lambda i: (i, 4)),
                  pl.BlockSpec((tc, width), lambda i: (i, 5)),
                  pl.BlockSpec((1, width), lambda i: (0, 0)),
                  pl.BlockSpec((1, width), lambda i: (0, 0)),
                  pl.BlockSpec((SG_GROUPS, CHUNK, CHUNK), lambda i: (0, 0, 0)),
                  pl.BlockSpec((CHUNK, width), lambda i: (0, 0))],
        out_specs=pl.BlockSpec((tc, width), lambda i: (i, 0)),
        out_shape=jax.ShapeDtypeStruct((T, width), BF16),
        compiler_params=_params("parallel"),
        name="chunk_spatial_gate",
    )(z, z, ln_g.reshape(1, width), ln_b.reshape(1, width), ws, bias)


def _merge_kernel(a_ref, p_ref, s_ref, gate_ref, x_ref, wb_ref, wo_ref, o_ref):
    d = x_ref.shape[1]
    merged = None
    for n, br in enumerate((a_ref, p_ref, s_ref)):
        proj = jnp.dot(br[...], wb_ref[n], preferred_element_type=F32)
        gate = jax.nn.sigmoid(gate_ref[:, n * d:(n + 1) * d].astype(F32))
        merged = gate * proj if merged is None else merged + gate * proj
    o_ref[...] = x_ref[...] + jnp.dot(merged.astype(BF16), wo_ref[...], preferred_element_type=F32)


def branch_merge(a_out, p_out, s_out, z, x, w_branch, w_out, *, tm):
    T, D = x.shape
    bw = a_out.shape[1]
    br_spec = pl.BlockSpec((tm, bw), lambda i: (i, 0))
    return pl.pallas_call(
        _merge_kernel,
        grid=(T // tm,),
        in_specs=[br_spec, br_spec, br_spec,
                  pl.BlockSpec((tm, N_BRANCH * D), lambda i: (i, 1)),
                  pl.BlockSpec((tm, D), lambda i: (i, 0)),
                  pl.BlockSpec((N_BRANCH, bw, D), lambda i: (0, 0, 0)),
                  pl.BlockSpec((D, D), lambda i: (0, 0))],
        out_specs=pl.BlockSpec((tm, D), lambda i: (i, 0)),
        out_shape=jax.ShapeDtypeStruct((T, D), F32),
        compiler_params=_params("parallel"),
        name="branch_merge",
    )(a_out, p_out, s_out, z, x, w_branch.astype(BF16), w_out.astype(BF16))


def _cross_kernel(x_ref, g_ref, k_ref, v_ref, wq_ref, wo_ref, o_ref):
    x = x_ref[...]
    d = x.shape[1]
    hd = d // X_HEADS
    xn = _rmsnorm_rows(x, g_ref[...]).astype(BF16)
    q = jnp.dot(xn, wq_ref[...], preferred_element_type=F32).astype(BF16)
    heads = []
    for h in range(X_HEADS):
        cols = slice(h * hd, (h + 1) * hd)
        s = lax.dot_general(q[:, cols], k_ref[:, cols], (((1,), (1,)), ((), ())), preferred_element_type=F32)
        s = s * (hd ** -0.5)
        p = jnp.exp(s - jnp.max(s, axis=1, keepdims=True))
        p = p / jnp.sum(p, axis=1, keepdims=True)
        heads.append(jnp.dot(p.astype(BF16), v_ref[:, cols], preferred_element_type=F32).astype(BF16))
    o = jnp.concatenate(heads, axis=1)
    o_ref[...] = x + jnp.dot(o, wo_ref[...], preferred_element_type=F32)


def cross_attention(x, g, kv, w_q, w_o, *, batch, seq, tm):
    T, D = x.shape
    M = kv.shape[0] // batch
    nb = seq // tm
    return pl.pallas_call(
        _cross_kernel,
        grid=(batch, nb),
        in_specs=[pl.BlockSpec((tm, D), lambda b, i: (b * nb + i, 0)),
                  pl.BlockSpec((1, D), lambda b, i: (0, 0)),
                  pl.BlockSpec((M, D), lambda b, i: (b, 0)),
                  pl.BlockSpec((M, D), lambda b, i: (b, 1)),
                  pl.BlockSpec((D, D), lambda b, i: (0, 0)),
                  pl.BlockSpec((D, D), lambda b, i: (0, 0))],
        out_specs=pl.BlockSpec((tm, D), lambda b, i: (b * nb + i, 0)),
        out_shape=jax.ShapeDtypeStruct((T, D), F32),
        compiler_params=_params("parallel", "parallel"),
        name="cross_attention",
    )(x, g.reshape(1, D), kv, kv, w_q.astype(BF16), w_o.astype(BF16))


def _peer_scores_kernel(x_ref, g_ref, wq_ref, keys_ref, o_ref):
    xn = _rmsnorm_rows(x_ref[...], g_ref[...]).astype(BF16)
    q = jnp.dot(xn, wq_ref[...], preferred_element_type=F32).astype(BF16)
    half = keys_ref.shape[2]
    for hp in range(keys_ref.shape[0]):
        o_ref[hp] = lax.dot_general(keys_ref[hp], q[:, hp * half:(hp + 1) * half],
                                    (((1,), (1,)), ((), ())), preferred_element_type=F32)


def peer_scores(x, g, w_pq, sub_keys, *, tm):
    T, D = x.shape
    wq = w_pq.reshape(D, -1).astype(BF16)
    keys = sub_keys.reshape(-1, sub_keys.shape[-2], sub_keys.shape[-1]).astype(BF16)
    nhp, nk, half = keys.shape
    return pl.pallas_call(
        _peer_scores_kernel,
        grid=(T // tm,),
        in_specs=[pl.BlockSpec((tm, D), lambda i: (i, 0)),
                  pl.BlockSpec((1, D), lambda i: (0, 0)),
                  pl.BlockSpec((D, nhp * half), lambda i: (0, 0)),
                  pl.BlockSpec((nhp, nk, half), lambda i: (0, 0, 0))],
        out_specs=pl.BlockSpec((nhp, nk, tm), lambda i: (0, 0, i)),
        out_shape=jax.ShapeDtypeStruct((nhp, nk, T), F32),
        compiler_params=_params("parallel"),
        name="peer_scores",
    )(x, g.reshape(1, D), wq, keys)


def _peer_route_kernel(s_ref, rank_ref, e2_ref, cnt_ref, c1_ref, work_ref, pos_ref, top_ref):
    nk = s_ref.shape[1]
    shape = s_ref.shape[1:]
    key_iota = lax.broadcasted_iota(jnp.int32, shape, 0)

    for side in range(2):
        work_ref[...] = s_ref[side]
        pos_ref[side] = jnp.full(shape, PEER_TOPK, jnp.int32)

        def extract(r, _, side=side):
            s = work_ref[...]
            m = jnp.max(s, axis=0)
            idx = jnp.min(jnp.where(s == m[None], key_iota, nk), axis=0)
            sel = key_iota == idx[None]
            pos_ref[side] = jnp.where(sel, r, pos_ref[side])
            work_ref[...] = jnp.where(sel, -jnp.inf, s)
            top_ref[side, r] = m
            return 0

        lax.fori_loop(0, PEER_TOPK, extract, 0)

    t0 = [top_ref[0, a] for a in range(PEER_TOPK)]
    t1 = [top_ref[1, b] for b in range(PEER_TOPK)]
    cand = [t0[a] + t1[b] for a, b in PEER_CELLS]
    best = cand[0]
    live = list(cand)
    chosen = [None] * len(PEER_CELLS)
    for _ in range(PEER_TOPK):
        m = functools.reduce(jnp.maximum, live)
        first = jnp.full(m.shape, len(PEER_CELLS), jnp.int32)
        for ci in reversed(range(len(PEER_CELLS))):
            first = jnp.where(live[ci] == m, ci, first)
        for ci in range(len(PEER_CELLS)):
            hit = first == ci
            chosen[ci] = hit if chosen[ci] is None else chosen[ci] | hit
            live[ci] = jnp.where(hit, -jnp.inf, live[ci])
    denom = None
    cnt_a = [jnp.zeros(best.shape, F32) for _ in range(PEER_TOPK)]
    for ci, (a, b) in enumerate(PEER_CELLS):
        term = jnp.where(chosen[ci], jnp.exp(cand[ci] - best), 0.0)
        denom = term if denom is None else denom + term
        cnt_a[a] = cnt_a[a] + jnp.where(chosen[ci], 1.0, 0.0)
    inv_denom = 1.0 / denom

    pos0 = pos_ref[0]
    cnt = jnp.zeros(shape, F32)
    for a in range(PEER_TOPK):
        cnt = jnp.where(pos0 == a, cnt_a[a][None], cnt)
    cnt_ref[...] = cnt
    c1_ref[...] = jnp.exp(s_ref[0] - t0[0][None]) * inv_denom[None]
    rank_ref[...] = pos_ref[1].astype(F32)
    e2_ref[...] = jnp.exp(s_ref[1] - t1[0][None])


def peer_route(scores, *, tb):
    _, nk, nh, T = scores.shape
    out = jax.ShapeDtypeStruct((nk, nh, T), F32)
    spec = pl.BlockSpec((nk, nh, tb), lambda i: (0, 0, i))
    return pl.pallas_call(
        _peer_route_kernel,
        grid=(T // tb,),
        in_specs=[pl.BlockSpec((2, nk, nh, tb), lambda i: (0, 0, 0, i))],
        out_specs=[spec] * 4,
        out_shape=[out] * 4,
        scratch_shapes=[pltpu.VMEM((nk, nh, tb), F32),
                        pltpu.VMEM((2, nk, nh, tb), jnp.int32),
                        pltpu.VMEM((2, PEER_TOPK, nh, tb), F32)],
        compiler_params=_params("parallel"),
        name="peer_route",
    )(scores)


def _peer_dense_kernel(x_ref, g_ref, u_ref, vt_ref, rank_ref, e2_ref, cnt_ref, c1_ref, o_ref, xnt_ref, acc_ref):
    e = pl.program_id(1)

    @pl.when(e == 0)
    def _():
        xn = _rmsnorm_rows(x_ref[...], g_ref[...])
        xnt_ref[...] = xn.T.astype(BF16)
        acc_ref[...] = jnp.zeros_like(acc_ref)

    h = jnp.dot(u_ref[...], xnt_ref[...], preferred_element_type=F32)
    act = 0.5 * h * (1.0 + lax.erf(h * (2.0 ** -0.5)))
    nk = rank_ref.shape[1]
    tiles = []
    for ii in range(u_ref.shape[0] // nk):
        w = None
        for hd in range(rank_ref.shape[0]):
            cnt = cnt_ref[ii, hd:hd + 1, :]
            c1 = c1_ref[ii, hd:hd + 1, :]
            term = jnp.where(rank_ref[hd] < cnt, e2_ref[hd] * c1, 0.0)
            w = term if w is None else w + term
        tiles.append((act[ii * nk:(ii + 1) * nk] * w).astype(BF16))
    a = jnp.concatenate(tiles, axis=0) if len(tiles) > 1 else tiles[0]
    acc_ref[...] += jnp.dot(vt_ref[...], a, preferred_element_type=F32)

    @pl.when(e == pl.num_programs(1) - 1)
    def _():
        o_ref[...] = x_ref[...] + acc_ref[...].T


def peer_dense(x, g, u, vt, rank, e2, cnt, c1, *, tb, te):
    T, D = x.shape
    E = u.shape[0]
    nh, nk, _ = rank.shape
    return pl.pallas_call(
        _peer_dense_kernel,
        grid=(T // tb, E // te),
        in_specs=[pl.BlockSpec((tb, D), lambda i, e: (i, 0)),
                  pl.BlockSpec((1, D), lambda i, e: (0, 0)),
                  pl.BlockSpec((te, D), lambda i, e: (e, 0)),
                  pl.BlockSpec((D, te), lambda i, e: (0, e)),
                  pl.BlockSpec((nh, nk, tb), lambda i, e: (0, 0, i)),
                  pl.BlockSpec((nh, nk, tb), lambda i, e: (0, 0, i)),
                  pl.BlockSpec((te // nk, nh, tb), lambda i, e: (e, 0, i)),
                  pl.BlockSpec((te // nk, nh, tb), lambda i, e: (e, 0, i))],
        out_specs=pl.BlockSpec((tb, D), lambda i, e: (i, 0)),
        out_shape=jax.ShapeDtypeStruct((T, D), F32),
        scratch_shapes=[pltpu.VMEM((D, tb), BF16), pltpu.VMEM((D, tb), F32)],
        compiler_params=_params("parallel", "arbitrary"),
        name="peer_dense",
    )(x, g.reshape(1, D), u, vt, rank, e2, cnt, c1)


def _final_norm_kernel(x_ref, g_ref, o_ref):
    o_ref[...] = _rmsnorm_rows(x_ref[...], g_ref[...])


def final_norm(x, g, *, tm):
    T, D = x.shape
    return pl.pallas_call(
        _final_norm_kernel,
        grid=(T // tm,),
        in_specs=[pl.BlockSpec((tm, D), lambda i: (i, 0)), pl.BlockSpec((1, D), lambda i: (0, 0))],
        out_specs=pl.BlockSpec((tm, D), lambda i: (i, 0)),
        out_shape=jax.ShapeDtypeStruct((T, D), F32),
        compiler_params=_params("parallel"),
        name="final_norm",
    )(x, g.reshape(1, D))


def _tile(n, want):
    if n <= want:
        return n
    t = want - want % LANES
    while n % t:
        t -= LANES
    return t


def kernel(x, mem, positions, g_mix, w_in, lam_qk, subln_g, w_pool, pool_scale, sg_ln_g, sg_ln_b, w_spatial,
           b_spatial, w_branch, w_out, g_cross, g_mem, w_xq, w_xkv, w_xo, g_ffn, w_pq, sub_keys, peer_u, peer_v,
           g_final):
    B, S, D = x.shape
    T = B * S
    depth = w_in.shape[0]
    n_keys = sub_keys.shape[3]
    x = x.reshape(T, D)
    memf = mem.reshape(-1, D)
    cos_tab, sin_tab, perm = rope_tables(positions)

    tm = _tile(S, 1024)
    tb_route = _tile(S, 128)
    tb_dense = _tile(S, 512)

    for l in range(depth):
        lam_init = 0.8 - 0.6 * math.exp(-0.3 * l)
        lq = lam_qk[l].astype(F32)
        lam = (jnp.exp(jnp.sum(lq[0] * lq[1])) - jnp.exp(jnp.sum(lq[2] * lq[3])) + lam_init).reshape(1)

        z = norm_matmul(x, g_mix[l], w_in[l].astype(BF16), tm=tm, tn=1536, name="in_proj")
        qk = rope_qk(z, cos_tab, sin_tab, perm, tm=tm)
        a_out = diff_attention(qk, z, lam, subln_g[l], batch=B, seq=S, tq=_tile(S, 512), lam_init=lam_init)
        p_out = multiscale_pool(z, w_pool[l], pool_scale[l], batch=B, seq=S, tc=_tile(S, 256))
        s_out = chunk_spatial_gate(z, sg_ln_g[l], sg_ln_b[l], w_spatial[l], b_spatial[l], tc=_tile(S, 512))
        x = branch_merge(a_out, p_out, s_out, z, x, w_branch[l], w_out[l], tm=_tile(S, 512))

        kv = norm_matmul(memf, g_mem[l], w_xkv[l].astype(BF16), tm=_tile(memf.shape[0], 1024), tn=1024,
                         name="mem_kv")
        x = cross_attention(x, g_cross[l], kv, w_xq[l], w_xo[l], batch=B, seq=S, tm=_tile(S, 512))

        scores = peer_scores(x, g_ffn[l], w_pq[l], sub_keys[l], tm=_tile(S, 512))
        scores = scores.reshape(PEER_HEADS, 2, n_keys, T).transpose(1, 2, 0, 3)
        rank, e2, cnt, c1 = peer_route(scores, tb=tb_route)
        x = peer_dense(x, g_ffn[l], peer_u[l].astype(BF16), peer_v[l].T.astype(BF16),
                       rank.transpose(1, 0, 2), e2.transpose(1, 0, 2), cnt, c1, tb=tb_dense, te=512)

    return final_norm(x, g_final, tm=tm).reshape(B, S, D)
```

```python
import functools
import math

import jax
import jax.numpy as jnp
from jax import lax
from jax.experimental import pallas as pl
from jax.experimental.pallas import tpu as pltpu

F32 = jnp.float32
BF16 = jnp.bfloat16

EPS = 1e-6
LANES = 128
BF16_ROWS = 16
VMEM_LIMIT_BYTES = 48 * 1024 * 1024

DA_HEADS = 4
DA_HEAD_DIM = 64
ROT_DIM = DA_HEAD_DIM // 4
ROPE_THETA = 500000.0
POOL_WINDOWS = (2, 4, 8, 16)
CHUNK = 128
SG_GROUPS = 4
N_BRANCH = 3
X_HEADS = 4
PEER_HEADS = 8
N_KEYS = 128
PEER_TOPK = 16
PEER_CELLS = tuple((a, b) for a in range(PEER_TOPK) for b in range(PEER_TOPK) if (a + 1) * (b + 1) <= PEER_TOPK)


def _params(*sem):
    return pltpu.CompilerParams(dimension_semantics=sem, vmem_limit_bytes=VMEM_LIMIT_BYTES)


def _rmsnorm_rows(x, g):
    ms = jnp.mean(x * x, axis=-1, keepdims=True)
    return x * lax.rsqrt(ms + EPS) * g


def _norm_matmul_kernel(x_ref, g_ref, w_ref, o_ref, xn_ref):
    @pl.when(pl.program_id(1) == 0)
    def _():
        xn_ref[...] = _rmsnorm_rows(x_ref[...], g_ref[...]).astype(BF16)

    o_ref[...] = jnp.dot(xn_ref[...], w_ref[...], preferred_element_type=F32).astype(o_ref.dtype)


def norm_matmul(x, g, w, *, tm, tn, name):
    T, D = x.shape
    N = w.shape[1]
    return pl.pallas_call(
        _norm_matmul_kernel,
        grid=(T // tm, N // tn),
        in_specs=[pl.BlockSpec((tm, D), lambda i, j: (i, 0)),
                  pl.BlockSpec((1, D), lambda i, j: (0, 0)),
                  pl.BlockSpec((D, tn), lambda i, j: (0, j))],
        out_specs=pl.BlockSpec((tm, tn), lambda i, j: (i, j)),
        out_shape=jax.ShapeDtypeStruct((T, N), BF16),
        scratch_shapes=[pltpu.VMEM((tm, D), BF16)],
        compiler_params=_params("parallel", "arbitrary"),
        name=name,
    )(x, g.reshape(1, D), w)


def _rope_kernel(z_ref, c_ref, s_ref, p_ref, o_ref, *, n_q_groups):
    c = c_ref[...]
    s = s_ref[...]
    for gi in range(z_ref.shape[1] // LANES):
        cols = slice(gi * LANES, (gi + 1) * LANES)
        xg = z_ref[:, cols]
        swapped = jnp.dot(xg, p_ref[...], preferred_element_type=F32)
        r = xg.astype(F32) * c + swapped * s
        if gi < n_q_groups:
            r = r * (DA_HEAD_DIM ** -0.5)
        o_ref[:, cols] = r.astype(o_ref.dtype)


def rope_qk(z, cos_tab, sin_tab, perm, *, tm):
    T = z.shape[0]
    width = 2 * DA_HEADS * 2 * DA_HEAD_DIM
    return pl.pallas_call(
        functools.partial(_rope_kernel, n_q_groups=width // (2 * LANES)),
        grid=(T // tm,),
        in_specs=[pl.BlockSpec((tm, width), lambda i: (i, 0)),
                  pl.BlockSpec((tm, LANES), lambda i: (i, 0)),
                  pl.BlockSpec((tm, LANES), lambda i: (i, 0)),
                  pl.BlockSpec((LANES, LANES), lambda i: (0, 0))],
        out_specs=pl.BlockSpec((tm, width), lambda i: (i, 0)),
        out_shape=jax.ShapeDtypeStruct((T, width), BF16),
        compiler_params=_params("parallel"),
        name="rope_qk",
    )(z, cos_tab, sin_tab, perm)


def rope_tables(positions):
    half = ROT_DIM // 2
    inv = ROPE_THETA ** (-jnp.arange(0, ROT_DIM, 2, dtype=F32) / ROT_DIM)
    ang = positions.astype(F32).reshape(-1, 1) * inv
    cos, sin = jnp.cos(ang), jnp.sin(ang)
    T = ang.shape[0]
    ones = jnp.ones((T, DA_HEAD_DIM - ROT_DIM), F32)
    zeros = jnp.zeros((T, DA_HEAD_DIM - ROT_DIM), F32)
    c64 = jnp.concatenate([cos, cos, ones], axis=1)
    s64 = jnp.concatenate([-sin, sin, zeros], axis=1)
    lane = jnp.arange(LANES)
    d = lane % DA_HEAD_DIM
    src = jnp.where(d < half, lane + half, jnp.where(d < ROT_DIM, lane - half, -1))
    perm = (lane[:, None] == src[None, :]).astype(BF16)
    return jnp.tile(c64, (1, 2)), jnp.tile(s64, (1, 2)), perm


def _diff_attn_kernel(lam_ref, q_ref, k_ref, v_ref, g_ref, o_ref, *, tq, lam_init):
    qi = pl.program_id(2)
    q = q_ref[...]
    lane = lax.broadcasted_iota(jnp.int32, q.shape, 1)
    zero = jnp.zeros_like(q)
    q_maps = (jnp.where(lane < DA_HEAD_DIM, q, zero), jnp.where(lane >= DA_HEAD_DIM, q, zero))
    row = lax.broadcasted_iota(jnp.int32, (tq, tq), 0)
    col = lax.broadcasted_iota(jnp.int32, (tq, tq), 1)

    def block(kb, carry, diagonal):
        start = pl.multiple_of(kb * tq, tq)
        k = k_ref[pl.ds(start, tq), :]
        v = v_ref[pl.ds(start, tq), :]
        new = []
        for mi in range(2):
            m, l, acc = carry[3 * mi:3 * mi + 3]
            s = lax.dot_general(q_maps[mi], k, (((1,), (1,)), ((), ())), preferred_element_type=F32)
            if diagonal:
                s = jnp.where(col <= row, s, -jnp.inf)
            m_new = jnp.maximum(m, jnp.max(s, axis=1, keepdims=True))
            p = jnp.exp(s - m_new)
            alpha = jnp.exp(m - m_new)
            l = alpha * l + jnp.sum(p, axis=1, keepdims=True)
            acc = alpha * acc + jnp.dot(p.astype(BF16), v, preferred_element_type=F32)
            new += [m_new, l, acc]
        return tuple(new)

    init = (jnp.full((tq, 1), -jnp.inf, F32), jnp.zeros((tq, 1), F32), jnp.zeros((tq, LANES), F32)) * 2
    carry = lax.fori_loop(0, qi, lambda kb, c: block(kb, c, False), init)
    m1, l1, a1, m2, l2, a2 = block(qi, carry, True)
    o = a1 / l1 - lam_ref[0] * (a2 / l2)
    o_ref[...] = (_rmsnorm_rows(o, g_ref[...]) * (1.0 - lam_init)).astype(o_ref.dtype)


def diff_attention(qk, z, lam, subln_g, *, batch, seq, tq, lam_init):
    T = qk.shape[0]
    nq = seq // tq
    width = DA_HEADS * 2 * DA_HEAD_DIM
    return pl.pallas_call(
        functools.partial(_diff_attn_kernel, tq=tq, lam_init=lam_init),
        grid=(batch, DA_HEADS, nq),
        in_specs=[pl.BlockSpec(memory_space=pltpu.SMEM),
                  pl.BlockSpec((tq, LANES), lambda b, h, i: (b * nq + i, h)),
                  pl.BlockSpec((seq, LANES), lambda b, h, i: (b, DA_HEADS + h)),
                  pl.BlockSpec((seq, LANES), lambda b, h, i: (b, 2 * DA_HEADS + h)),
                  pl.BlockSpec((1, LANES), lambda b, h, i: (0, 0))],
        out_specs=pl.BlockSpec((tq, LANES), lambda b, h, i: (b * nq + i, h)),
        out_shape=jax.ShapeDtypeStruct((T, width), BF16),
        compiler_params=_params("parallel", "parallel", "arbitrary"),
        name="diff_attention",
    )(lam, qk, qk, z, subln_g.reshape(1, LANES))


def _pool_kernel(p_ref, prev_ref, cur_band_ref, prev_band_ref, inv_ref, w_ref, scale_ref, o_ref):
    has_prev = pl.program_id(1) > 0
    for g in range(len(POOL_WINDOWS)):
        cols = slice(g * LANES, (g + 1) * LANES)
        pg = p_ref[:, cols]
        wsum = jnp.dot(cur_band_ref[g], pg, preferred_element_type=F32)
        halo = jnp.dot(prev_band_ref[g], prev_ref[:, cols], preferred_element_type=F32)
        wsum = wsum + jnp.where(has_prev, halo, 0.0)
        d = wsum * inv_ref[:, cols] - pg.astype(F32)
        y = jnp.dot(d.astype(BF16), w_ref[g], preferred_element_type=F32) * scale_ref[:, cols]
        o_ref[:, cols] = y.astype(o_ref.dtype)


def multiscale_pool(z, w_pool, pool_scale, *, batch, seq, tc):
    T = z.shape[0]
    nc = seq // tc
    width = len(POOL_WINDOWS) * LANES
    col_block = 3
    t = jnp.arange(tc)
    delta = t[:, None] - t[None, :]
    win = jnp.asarray(POOL_WINDOWS)[:, None, None]
    cur_band = ((delta >= 0) & (delta < win)).astype(BF16)
    prev_band = ((delta + tc >= 0) & (delta + tc < win)).astype(BF16)
    pos = jnp.arange(seq)
    inv_cnt = 1.0 / jnp.minimum(pos[:, None] + 1, jnp.repeat(jnp.asarray(POOL_WINDOWS), LANES)[None, :]).astype(F32)
    return pl.pallas_call(
        _pool_kernel,
        grid=(batch, nc),
        in_specs=[pl.BlockSpec((tc, width), lambda b, c: (b * nc + c, col_block)),
                  pl.BlockSpec((tc, width), lambda b, c: (b * nc + jnp.maximum(c - 1, 0), col_block)),
                  pl.BlockSpec((len(POOL_WINDOWS), tc, tc), lambda b, c: (0, 0, 0)),
                  pl.BlockSpec((len(POOL_WINDOWS), tc, tc), lambda b, c: (0, 0, 0)),
                  pl.BlockSpec((tc, width), lambda b, c: (c, 0)),
                  pl.BlockSpec((len(POOL_WINDOWS), LANES, LANES), lambda b, c: (0, 0, 0)),
                  pl.BlockSpec((1, width), lambda b, c: (0, 0))],
        out_specs=pl.BlockSpec((tc, width), lambda b, c: (b * nc + c, 0)),
        out_shape=jax.ShapeDtypeStruct((T, width), BF16),
        compiler_params=_params("parallel", "arbitrary"),
        name="multiscale_pool",
    )(z, z, cur_band, prev_band, inv_cnt, w_pool.astype(BF16), pool_scale.reshape(1, width))


def _sgate_kernel(u_ref, v_ref, g_ref, b_ref, ws_ref, bias_ref, o_ref):
    v = v_ref[...].astype(F32)
    mu = jnp.mean(v, axis=-1, keepdims=True)
    vc = v - mu
    var = jnp.mean(vc * vc, axis=-1, keepdims=True)
    vn = (vc * lax.rsqrt(var + EPS) * g_ref[...] + b_ref[...]).astype(BF16)
    for n in range(v.shape[0] // CHUNK):
        rows = slice(n * CHUNK, (n + 1) * CHUNK)
        for g in range(SG_GROUPS):
            cols = slice(g * LANES, (g + 1) * LANES)
            mixed = jnp.dot(ws_ref[g], vn[rows, cols], preferred_element_type=F32) + bias_ref[:, cols]
            o_ref[rows, cols] = (u_ref[rows, cols].astype(F32) * mixed).astype(o_ref.dtype)


def chunk_spatial_gate(z, ln_g, ln_b, w_s, b_s, *, tc):
    T = z.shape[0]
    width = SG_GROUPS * LANES
    ws = (w_s * jnp.tril(jnp.ones((CHUNK, CHUNK), w_s.dtype))).astype(BF16)
    bias = jnp.repeat(b_s.T, LANES, axis=1)
    return pl.pallas_call(
        _sgate_kernel,
        grid=(T // tc,),
        in_specs=[pl.BlockSpec((tc, width), lambda i: (i, 4)),
                  pl.BlockSpec((tc, width), lambda i: (i, 5)),
                  pl.BlockSpec((1, width), lambda i: (0, 0)),
                  pl.BlockSpec((1, width), lambda i: (0, 0)),
                  pl.BlockSpec((SG_GROUPS, CHUNK, CHUNK), lambda i: (0, 0, 0)),
                  pl.BlockSpec((CHUNK, width), lambda i: (0, 0))],
        out_specs=pl.BlockSpec((tc, width), lambda i: (i, 0)),
        out_shape=jax.ShapeDtypeStruct((T, width), BF16),
        compiler_params=_params("parallel"),
        name="chunk_spatial_gate",
    )(z, z, ln_g.reshape(1, width), ln_b.reshape(1, width), ws, bias)


def _merge_kernel(a_ref, p_ref, s_ref, gate_ref, x_ref, wb_ref, wo_ref, o_ref):
    d = x_ref.shape[1]
    merged = None
    for n, br in enumerate((a_ref, p_ref, s_ref)):
        proj = jnp.dot(br[...], wb_ref[n], preferred_element_type=F32)
        gate = jax.nn.sigmoid(gate_ref[:, n * d:(n + 1) * d].astype(F32))
        merged = gate * proj if merged is None else merged + gate * proj
    o_ref[...] = x_ref[...] + jnp.dot(merged.astype(BF16), wo_ref[...], preferred_element_type=F32)


def branch_merge(a_out, p_out, s_out, z, x, w_branch, w_out, *, tm):
    T, D = x.shape
    bw = a_out.shape[1]
    br_spec = pl.BlockSpec((tm, bw), lambda i: (i, 0))
    return pl.pallas_call(
        _merge_kernel,
        grid=(T // tm,),
        in_specs=[br_spec, br_spec, br_spec,
                  pl.BlockSpec((tm, N_BRANCH * D), lambda i: (i, 1)),
                  pl.BlockSpec((tm, D), lambda i: (i, 0)),
                  pl.BlockSpec((N_BRANCH, bw, D), lambda i: (0, 0, 0)),
                  pl.BlockSpec((D, D), lambda i: (0, 0))],
        out_specs=pl.BlockSpec((tm, D), lambda i: (i, 0)),
        out_shape=jax.ShapeDtypeStruct((T, D), F32),
        compiler_params=_params("parallel"),
        name="branch_merge",
    )(a_out, p_out, s_out, z, x, w_branch.astype(BF16), w_out.astype(BF16))


def _cross_kernel(x_ref, g_ref, k_ref, v_ref, wq_ref, wo_ref, o_ref):
    x = x_ref[...]
    d = x.shape[1]
    hd = d // X_HEADS
    xn = _rmsnorm_rows(x, g_ref[...]).astype(BF16)
    q = jnp.dot(xn, wq_ref[...], preferred_element_type=F32).astype(BF16)
    heads = []
    for h in range(X_HEADS):
        cols = slice(h * hd, (h + 1) * hd)
        s = lax.dot_general(q[:, cols], k_ref[:, cols], (((1,), (1,)), ((), ())), preferred_element_type=F32)
        s = s * (hd ** -0.5)
        p = jnp.exp(s - jnp.max(s, axis=1, keepdims=True))
        p = p / jnp.sum(p, axis=1, keepdims=True)
        heads.append(jnp.dot(p.astype(BF16), v_ref[:, cols], preferred_element_type=F32).astype(BF16))
    o = jnp.concatenate(heads, axis=1)
    o_ref[...] = x + jnp.dot(o, wo_ref[...], preferred_element_type=F32)


def cross_attention(x, g, kv, w_q, w_o, *, batch, seq, tm):
    T, D = x.shape
    M = kv.shape[0] // batch
    nb = seq // tm
    return pl.pallas_call(
        _cross_kernel,
        grid=(batch, nb),
        in_specs=[pl.BlockSpec((tm, D), lambda b, i: (b * nb + i, 0)),
                  pl.BlockSpec((1, D), lambda b, i: (0, 0)),
                  pl.BlockSpec((M, D), lambda b, i: (b, 0)),
                  pl.BlockSpec((M, D), lambda b, i: (b, 1)),
                  pl.BlockSpec((D, D), lambda b, i: (0, 0)),
                  pl.BlockSpec((D, D), lambda b, i: (0, 0))],
        out_specs=pl.BlockSpec((tm, D), lambda b, i: (b * nb + i, 0)),
        out_shape=jax.ShapeDtypeStruct((T, D), F32),
        compiler_params=_params("parallel", "parallel"),
        name="cross_attention",
    )(x, g.reshape(1, D), kv, kv, w_q.astype(BF16), w_o.astype(BF16))


def _peer_scores_kernel(x_ref, g_ref, wq_ref, keys_ref, o_ref):
    xn = _rmsnorm_rows(x_ref[...], g_ref[...]).astype(BF16)
    q = jnp.dot(xn, wq_ref[...], preferred_element_type=F32).astype(BF16)
    half = keys_ref.shape[2]
    for hp in range(keys_ref.shape[0]):
        o_ref[hp] = lax.dot_general(keys_ref[hp], q[:, hp * half:(hp + 1) * half],
                                    (((1,), (1,)), ((), ())), preferred_element_type=F32)


def peer_scores(x, g, w_pq, sub_keys, *, tm):
    T, D = x.shape
    wq = w_pq.reshape(D, -1).astype(BF16)
    keys = sub_keys.reshape(-1, sub_keys.shape[-2], sub_keys.shape[-1]).astype(BF16)
    nhp, nk, half = keys.shape
    return pl.pallas_call(
        _peer_scores_kernel,
        grid=(T // tm,),
        in_specs=[pl.BlockSpec((tm, D), lambda i: (i, 0)),
                  pl.BlockSpec((1, D), lambda i: (0, 0)),
                  pl.BlockSpec((D, nhp * half), lambda i: (0, 0)),
                  pl.BlockSpec((nhp, nk, half), lambda i: (0, 0, 0))],
        out_specs=pl.BlockSpec((nhp, nk, tm), lambda i: (0, 0, i)),
        out_shape=jax.ShapeDtypeStruct((nhp, nk, T), F32),
        compiler_params=_params("parallel"),
        name="peer_scores",
    )(x, g.reshape(1, D), wq, keys)


def _tree_reduce(op, xs):
    xs = [xs[i] for i in range(len(xs))]
    while len(xs) > 1:
        half = (len(xs) + 1) // 2
        xs = [op(xs[i], xs[i + half]) if i + half < len(xs) else xs[i] for i in range(half)]
    return xs[0]


def _peer_route_kernel(s_ref, rank_ref, e2_ref, cnt_ref, c1_ref, work_ref, top_ref, arg_ref):
    nk = s_ref.shape[1]
    shape = s_ref.shape[1:]
    key_iota = lax.broadcasted_iota(jnp.int32, shape, 0).astype(F32)

    for side in range(2):
        work_ref[...] = s_ref[side]

        def extract(r, _, side=side):
            s = work_ref[...]
            m = _tree_reduce(jnp.maximum, s)
            idx = _tree_reduce(jnp.minimum, jnp.where(s == m[None], key_iota, float(nk)))
            work_ref[...] = jnp.where(key_iota == idx[None], -jnp.inf, s)
            top_ref[side, r] = m
            arg_ref[side, r] = idx
            return 0

        lax.fori_loop(0, PEER_TOPK, extract, 0)

    t0 = [top_ref[0, a] for a in range(PEER_TOPK)]
    t1 = [top_ref[1, b] for b in range(PEER_TOPK)]
    cand = [t0[a] + t1[b] for a, b in PEER_CELLS]
    best = cand[0]
    live = list(cand)
    chosen = [None] * len(PEER_CELLS)
    for _ in range(PEER_TOPK):
        m = _tree_reduce(jnp.maximum, live)
        first = _tree_reduce(jnp.minimum, [jnp.where(live[ci] == m, float(ci), float(len(PEER_CELLS)))
                                           for ci in range(len(PEER_CELLS))])
        for ci in range(len(PEER_CELLS)):
            hit = first == float(ci)
            chosen[ci] = hit if chosen[ci] is None else chosen[ci] | hit
            live[ci] = jnp.where(hit, -jnp.inf, live[ci])
    denom = None
    cnt_a = [jnp.zeros(best.shape, F32) for _ in range(PEER_TOPK)]
    for ci, (a, b) in enumerate(PEER_CELLS):
        term = jnp.where(chosen[ci], jnp.exp(cand[ci] - best), 0.0)
        denom = term if denom is None else denom + term
        cnt_a[a] = cnt_a[a] + jnp.where(chosen[ci], 1.0, 0.0)
    inv_denom = 1.0 / denom

    cnt = jnp.zeros(shape, F32)
    rank = jnp.full(shape, float(PEER_TOPK), F32)
    for r in range(PEER_TOPK):
        cnt = jnp.where(key_iota == arg_ref[0, r][None], cnt_a[r][None], cnt)
        rank = jnp.where(key_iota == arg_ref[1, r][None], float(r), rank)
    cnt_ref[...] = cnt
    c1_ref[...] = jnp.exp(s_ref[0] - t0[0][None]) * (0.5 * inv_denom)[None]
    rank_ref[...] = rank
    e2_ref[...] = jnp.exp(s_ref[1] - t1[0][None])


def peer_route(scores):
    _, nk, nh, T = scores.shape
    tb = LANES
    spec = pl.BlockSpec((nk, nh, tb), lambda i: (0, 0, i))
    out = jax.ShapeDtypeStruct((nk, nh, T), F32)
    return pl.pallas_call(
        _peer_route_kernel,
        grid=(T // tb,),
        in_specs=[pl.BlockSpec((2, nk, nh, tb), lambda i: (0, 0, 0, i))],
        out_specs=[spec] * 4,
        out_shape=[out] * 4,
        scratch_shapes=[pltpu.VMEM((nk, nh, tb), F32),
                        pltpu.VMEM((2, PEER_TOPK, nh, tb), F32),
                        pltpu.VMEM((2, PEER_TOPK, nh, tb), F32)],
        compiler_params=_params("parallel"),
        name="peer_route",
    )(scores)


def _peer_dense_kernel(x_ref, g_ref, u_ref, vt_ref, rank_ref, e2_ref, cnt_ref, c1_ref, o_ref, xnt_ref, acc_ref):
    e = pl.program_id(1)

    @pl.when(e == 0)
    def _():
        xn = _rmsnorm_rows(x_ref[...], g_ref[...])
        xnt_ref[...] = xn.T.astype(BF16)
        acc_ref[...] = jnp.zeros_like(acc_ref)

    h = jnp.dot(u_ref[...], xnt_ref[...], preferred_element_type=F32)
    act = (h * (1.0 + lax.erf(h * (2.0 ** -0.5)))).astype(BF16)
    nh, groups, rows, tb = rank_ref.shape
    n_tiles = u_ref.shape[0] // (groups * rows)
    act = act.reshape(n_tiles, groups, rows, tb)
    zero = jnp.zeros((groups, rows, tb), BF16)
    w = [None] * n_tiles
    for hd in range(nh):
        rank = rank_ref[hd]
        e2 = e2_ref[hd]
        for ii in range(n_tiles):
            cnt = jnp.broadcast_to(cnt_ref[ii, hd:hd + 1, :], (rows, tb)).astype(BF16)
            c1 = jnp.broadcast_to(c1_ref[ii, hd:hd + 1, :], (rows, tb)).astype(BF16)
            term = jnp.where(rank < cnt[None], e2 * c1[None], zero)
            w[ii] = term if w[ii] is None else w[ii] + term
    a = jnp.stack([act[ii] * w[ii] for ii in range(n_tiles)], axis=0).reshape(u_ref.shape[0], tb)
    acc_ref[...] += jnp.dot(vt_ref[...], a, preferred_element_type=F32)

    @pl.when(e == pl.num_programs(1) - 1)
    def _():
        o_ref[...] = x_ref[...] + acc_ref[...].T


def peer_dense(x, g, u, vt, rank, e2, cnt, c1, *, tb, te):
    T, D = x.shape
    E = u.shape[0]
    nh, nk, _ = rank.shape
    rank = rank.reshape(nh, nk // BF16_ROWS, BF16_ROWS, T)
    e2 = e2.reshape(nh, nk // BF16_ROWS, BF16_ROWS, T)
    tab_spec = pl.BlockSpec((nh, nk // BF16_ROWS, BF16_ROWS, tb), lambda i, e: (0, 0, 0, i))
    row_spec = pl.BlockSpec((te // nk, nh, tb), lambda i, e: (e, 0, i))
    return pl.pallas_call(
        _peer_dense_kernel,
        grid=(T // tb, E // te),
        in_specs=[pl.BlockSpec((tb, D), lambda i, e: (i, 0)),
                  pl.BlockSpec((1, D), lambda i, e: (0, 0)),
                  pl.BlockSpec((te, D), lambda i, e: (e, 0)),
                  pl.BlockSpec((D, te), lambda i, e: (0, e)),
                  tab_spec, tab_spec, row_spec, row_spec],
        out_specs=pl.BlockSpec((tb, D), lambda i, e: (i, 0)),
        out_shape=jax.ShapeDtypeStruct((T, D), F32),
        scratch_shapes=[pltpu.VMEM((D, tb), BF16), pltpu.VMEM((D, tb), F32)],
        compiler_params=_params("parallel", "arbitrary"),
        name="peer_dense",
    )(x, g.reshape(1, D), u, vt, rank, e2, cnt, c1)


def _final_norm_kernel(x_ref, g_ref, o_ref):
    o_ref[...] = _rmsnorm_rows(x_ref[...], g_ref[...])


def final_norm(x, g, *, tm):
    T, D = x.shape
    return pl.pallas_call(
        _final_norm_kernel,
        grid=(T // tm,),
        in_specs=[pl.BlockSpec((tm, D), lambda i: (i, 0)), pl.BlockSpec((1, D), lambda i: (0, 0))],
        out_specs=pl.BlockSpec((tm, D), lambda i: (i, 0)),
        out_shape=jax.ShapeDtypeStruct((T, D), F32),
        compiler_params=_params("parallel"),
        name="final_norm",
    )(x, g.reshape(1, D))


def _tile(n, want):
    if n <= want:
        return n
    t = want - want % LANES
    while n % t:
        t -= LANES
    return t


def kernel(x, mem, positions, g_mix, w_in, lam_qk, subln_g, w_pool, pool_scale, sg_ln_g, sg_ln_b, w_spatial,
           b_spatial, w_branch, w_out, g_cross, g_mem, w_xq, w_xkv, w_xo, g_ffn, w_pq, sub_keys, peer_u, peer_v,
           g_final):
    B, S, D = x.shape
    T = B * S
    depth = w_in.shape[0]
    n_keys = sub_keys.shape[3]
    x = x.reshape(T, D)
    memf = mem.reshape(-1, D)
    cos_tab, sin_tab, perm = rope_tables(positions)

    tm = _tile(S, 1024)
    tb_dense = _tile(S, 512)

    for l in range(depth):
        lam_init = 0.8 - 0.6 * math.exp(-0.3 * l)
        lq = lam_qk[l].astype(F32)
        lam = (jnp.exp(jnp.sum(lq[0] * lq[1])) - jnp.exp(jnp.sum(lq[2] * lq[3])) + lam_init).reshape(1)

        z = norm_matmul(x, g_mix[l], w_in[l].astype(BF16), tm=tm, tn=1536, name="in_proj")
        qk = rope_qk(z, cos_tab, sin_tab, perm, tm=tm)
        a_out = diff_attention(qk, z, lam, subln_g[l], batch=B, seq=S, tq=_tile(S, 512), lam_init=lam_init)
        p_out = multiscale_pool(z, w_pool[l], pool_scale[l], batch=B, seq=S, tc=_tile(S, 256))
        s_out = chunk_spatial_gate(z, sg_ln_g[l], sg_ln_b[l], w_spatial[l], b_spatial[l], tc=_tile(S, 512))
        x = branch_merge(a_out, p_out, s_out, z, x, w_branch[l], w_out[l], tm=_tile(S, 512))

        kv = norm_matmul(memf, g_mem[l], w_xkv[l].astype(BF16), tm=_tile(memf.shape[0], 1024), tn=1024,
                         name="mem_kv")
        x = cross_attention(x, g_cross[l], kv, w_xq[l], w_xo[l], batch=B, seq=S, tm=_tile(S, 512))

        scores = peer_scores(x, g_ffn[l], w_pq[l], sub_keys[l], tm=_tile(S, 512))
        scores = scores.reshape(PEER_HEADS, 2, n_keys, T).transpose(1, 2, 0, 3)
        rank, e2, cnt, c1 = peer_route(scores)
        x = peer_dense(x, g_ffn[l], peer_u[l].astype(BF16), peer_v[l].T.astype(BF16),
                       rank.transpose(1, 0, 2).astype(BF16), e2.transpose(1, 0, 2).astype(BF16), cnt, c1,
                       tb=tb_dense, te=1024)

    return final_norm(x, g_final, tm=tm).reshape(B, S, D)
```

```python
import functools
import math

import jax
import jax.numpy as jnp
from jax import lax
from jax.experimental import pallas as pl
from jax.experimental.pallas import tpu as pltpu

F32 = jnp.float32
BF16 = jnp.bfloat16

EPS = 1e-6
LANES = 128
BF16_ROWS = 16
VMEM_LIMIT_BYTES = 48 * 1024 * 1024

DA_HEADS = 4
DA_HEAD_DIM = 64
ROT_DIM = DA_HEAD_DIM // 4
ROPE_THETA = 500000.0
POOL_WINDOWS = (2, 4, 8, 16)
CHUNK = 128
SG_GROUPS = 4
N_BRANCH = 3
X_HEADS = 4
PEER_HEADS = 8
N_KEYS = 128
PEER_TOPK = 16
PEER_CELLS = tuple((a, b) for a in range(PEER_TOPK) for b in range(PEER_TOPK) if (a + 1) * (b + 1) <= PEER_TOPK)


def _params(*sem):
    return pltpu.CompilerParams(dimension_semantics=sem, vmem_limit_bytes=VMEM_LIMIT_BYTES)


def _rmsnorm_rows(x, g):
    ms = jnp.mean(x * x, axis=-1, keepdims=True)
    return x * lax.rsqrt(ms + EPS) * g


def _norm_matmul_kernel(x_ref, g_ref, w_ref, o_ref, xn_ref):
    @pl.when(pl.program_id(1) == 0)
    def _():
        xn_ref[...] = _rmsnorm_rows(x_ref[...], g_ref[...]).astype(BF16)

    o_ref[...] = jnp.dot(xn_ref[...], w_ref[...], preferred_element_type=F32).astype(o_ref.dtype)


def norm_matmul(x, g, w, *, tm, tn, name):
    T, D = x.shape
    N = w.shape[1]
    return pl.pallas_call(
        _norm_matmul_kernel,
        grid=(T // tm, N // tn),
        in_specs=[pl.BlockSpec((tm, D), lambda i, j: (i, 0)),
                  pl.BlockSpec((1, D), lambda i, j: (0, 0)),
                  pl.BlockSpec((D, tn), lambda i, j: (0, j))],
        out_specs=pl.BlockSpec((tm, tn), lambda i, j: (i, j)),
        out_shape=jax.ShapeDtypeStruct((T, N), BF16),
        scratch_shapes=[pltpu.VMEM((tm, D), BF16)],
        compiler_params=_params("parallel", "arbitrary"),
        name=name,
    )(x, g.reshape(1, D), w)


def _rope_kernel(z_ref, c_ref, s_ref, p_ref, o_ref, *, n_q_groups):
    c = c_ref[...]
    s = s_ref[...]
    for gi in range(z_ref.shape[1] // LANES):
        cols = slice(gi * LANES, (gi + 1) * LANES)
        xg = z_ref[:, cols]
        swapped = jnp.dot(xg, p_ref[...], preferred_element_type=F32)
        r = xg.astype(F32) * c + swapped * s
        if gi < n_q_groups:
            r = r * (DA_HEAD_DIM ** -0.5)
        o_ref[:, cols] = r.astype(o_ref.dtype)


def rope_qk(z, cos_tab, sin_tab, perm, *, tm):
    T = z.shape[0]
    width = 2 * DA_HEADS * 2 * DA_HEAD_DIM
    return pl.pallas_call(
        functools.partial(_rope_kernel, n_q_groups=width // (2 * LANES)),
        grid=(T // tm,),
        in_specs=[pl.BlockSpec((tm, width), lambda i: (i, 0)),
                  pl.BlockSpec((tm, LANES), lambda i: (i, 0)),
                  pl.BlockSpec((tm, LANES), lambda i: (i, 0)),
                  pl.BlockSpec((LANES, LANES), lambda i: (0, 0))],
        out_specs=pl.BlockSpec((tm, width), lambda i: (i, 0)),
        out_shape=jax.ShapeDtypeStruct((T, width), BF16),
        compiler_params=_params("parallel"),
        name="rope_qk",
    )(z, cos_tab, sin_tab, perm)


def rope_tables(positions):
    half = ROT_DIM // 2
    inv = ROPE_THETA ** (-jnp.arange(0, ROT_DIM, 2, dtype=F32) / ROT_DIM)
    ang = positions.astype(F32).reshape(-1, 1) * inv
    cos, sin = jnp.cos(ang), jnp.sin(ang)
    T = ang.shape[0]
    ones = jnp.ones((T, DA_HEAD_DIM - ROT_DIM), F32)
    zeros = jnp.zeros((T, DA_HEAD_DIM - ROT_DIM), F32)
    c64 = jnp.concatenate([cos, cos, ones], axis=1)
    s64 = jnp.concatenate([-sin, sin, zeros], axis=1)
    lane = jnp.arange(LANES)
    d = lane % DA_HEAD_DIM
    src = jnp.where(d < half, lane + half, jnp.where(d < ROT_DIM, lane - half, -1))
    perm = (lane[:, None] == src[None, :]).astype(BF16)
    return jnp.tile(c64, (1, 2)), jnp.tile(s64, (1, 2)), perm


def _diff_attn_kernel(lam_ref, q_ref, k_ref, v_ref, g_ref, o_ref, *, tq, tk, lam_init):
    qi = pl.program_id(2)
    q = q_ref[...]
    lane = lax.broadcasted_iota(jnp.int32, q.shape, 1)
    zero = jnp.zeros_like(q)
    q_maps = (jnp.where(lane < DA_HEAD_DIM, q, zero), jnp.where(lane >= DA_HEAD_DIM, q, zero))
    row = lax.broadcasted_iota(jnp.int32, (tq, tk), 0)
    col = lax.broadcasted_iota(jnp.int32, (tq, tk), 1)
    per_q = tq // tk

    def block(kb, carry, diag_offset):
        start = pl.multiple_of(kb * tk, tk)
        k = k_ref[pl.ds(start, tk), :]
        v = v_ref[pl.ds(start, tk), :]
        new = []
        for mi in range(2):
            m, l, acc = carry[3 * mi:3 * mi + 3]
            s = lax.dot_general(q_maps[mi], k, (((1,), (1,)), ((), ())), preferred_element_type=F32)
            if diag_offset is not None:
                s = jnp.where(col + diag_offset <= row, s, -jnp.inf)
            m_new = jnp.maximum(m, jnp.max(s, axis=1, keepdims=True))
            p = jnp.exp(s - m_new)
            alpha = jnp.exp(m - m_new)
            l = alpha * l + jnp.sum(p, axis=1, keepdims=True)
            acc = alpha * acc + jnp.dot(p.astype(BF16), v, preferred_element_type=F32)
            new += [m_new, l, acc]
        return tuple(new)

    init = (jnp.full((tq, 1), -jnp.inf, F32), jnp.zeros((tq, 1), F32), jnp.zeros((tq, LANES), F32)) * 2
    carry = lax.fori_loop(0, qi * per_q, lambda kb, c: block(kb, c, None), init)
    for d in range(per_q):
        carry = block(qi * per_q + d, carry, d * tk)
    m1, l1, a1, m2, l2, a2 = carry
    o = a1 / l1 - lam_ref[0] * (a2 / l2)
    o_ref[...] = (_rmsnorm_rows(o, g_ref[...]) * (1.0 - lam_init)).astype(o_ref.dtype)


def diff_attention(qk, z, lam, subln_g, *, batch, seq, tq, tk, lam_init):
    T = qk.shape[0]
    nq = seq // tq
    width = DA_HEADS * 2 * DA_HEAD_DIM
    return pl.pallas_call(
        functools.partial(_diff_attn_kernel, tq=tq, tk=tk, lam_init=lam_init),
        grid=(batch, DA_HEADS, nq),
        in_specs=[pl.BlockSpec(memory_space=pltpu.SMEM),
                  pl.BlockSpec((tq, LANES), lambda b, h, i: (b * nq + i, h)),
                  pl.BlockSpec((seq, LANES), lambda b, h, i: (b, DA_HEADS + h)),
                  pl.BlockSpec((seq, LANES), lambda b, h, i: (b, 2 * DA_HEADS + h)),
                  pl.BlockSpec((1, LANES), lambda b, h, i: (0, 0))],
        out_specs=pl.BlockSpec((tq, LANES), lambda b, h, i: (b * nq + i, h)),
        out_shape=jax.ShapeDtypeStruct((T, width), BF16),
        compiler_params=_params("parallel", "parallel", "arbitrary"),
        name="diff_attention",
    )(lam, qk, qk, z, subln_g.reshape(1, LANES))


def _pool_kernel(p_ref, prev_ref, cur_band_ref, prev_band_ref, inv_ref, w_ref, scale_ref, o_ref):
    has_prev = pl.program_id(1) > 0
    for g in range(len(POOL_WINDOWS)):
        cols = slice(g * LANES, (g + 1) * LANES)
        pg = p_ref[:, cols]
        wsum = jnp.dot(cur_band_ref[g], pg, preferred_element_type=F32)
        halo = jnp.dot(prev_band_ref[g], prev_ref[:, cols], preferred_element_type=F32)
        wsum = wsum + jnp.where(has_prev, halo, 0.0)
        d = wsum * inv_ref[:, cols] - pg.astype(F32)
        y = jnp.dot(d.astype(BF16), w_ref[g], preferred_element_type=F32) * scale_ref[:, cols]
        o_ref[:, cols] = y.astype(o_ref.dtype)


def multiscale_pool(z, w_pool, pool_scale, *, batch, seq, tc):
    T = z.shape[0]
    nc = seq // tc
    width = len(POOL_WINDOWS) * LANES
    col_block = 3
    t = jnp.arange(tc)
    delta = t[:, None] - t[None, :]
    win = jnp.asarray(POOL_WINDOWS)[:, None, None]
    cur_band = ((delta >= 0) & (delta < win)).astype(BF16)
    prev_band = ((delta + tc >= 0) & (delta + tc < win)).astype(BF16)
    pos = jnp.arange(seq)
    inv_cnt = 1.0 / jnp.minimum(pos[:, None] + 1, jnp.repeat(jnp.asarray(POOL_WINDOWS), LANES)[None, :]).astype(F32)
    return pl.pallas_call(
        _pool_kernel,
        grid=(batch, nc),
        in_specs=[pl.BlockSpec((tc, width), lambda b, c: (b * nc + c, col_block)),
                  pl.BlockSpec((tc, width), lambda b, c: (b * nc + jnp.maximum(c - 1, 0), col_block)),
                  pl.BlockSpec((len(POOL_WINDOWS), tc, tc), lambda b, c: (0, 0, 0)),
                  pl.BlockSpec((len(POOL_WINDOWS), tc, tc), lambda b, c: (0, 0, 0)),
                  pl.BlockSpec((tc, width), lambda b, c: (c, 0)),
                  pl.BlockSpec((len(POOL_WINDOWS), LANES, LANES), lambda b, c: (0, 0, 0)),
                  pl.BlockSpec((1, width), lambda b, c: (0, 0))],
        out_specs=pl.BlockSpec((tc, width), lambda b, c: (b * nc + c, 0)),
        out_shape=jax.ShapeDtypeStruct((T, width), BF16),
        compiler_params=_params("parallel", "arbitrary"),
        name="multiscale_pool",
    )(z, z, cur_band, prev_band, inv_cnt, w_pool.astype(BF16), pool_scale.reshape(1, width))


def _sgate_kernel(u_ref, v_ref, g_ref, b_ref, ws_ref, bias_ref, o_ref):
    v = v_ref[...].astype(F32)
    mu = jnp.mean(v, axis=-1, keepdims=True)
    vc = v - mu
    var = jnp.mean(vc * vc, axis=-1, keepdims=True)
    vn = (vc * lax.rsqrt(var + EPS) * g_ref[...] + b_ref[...]).astype(BF16)
    for n in range(v.shape[0] // CHUNK):
        rows = slice(n * CHUNK, (n + 1) * CHUNK)
        for g in range(SG_GROUPS):
            cols = slice(g * LANES, (g + 1) * LANES)
            mixed = jnp.dot(ws_ref[g], vn[rows, cols], preferred_element_type=F32) + bias_ref[:, cols]
            o_ref[rows, cols] = (u_ref[rows, cols].astype(F32) * mixed).astype(o_ref.dtype)


def chunk_spatial_gate(z, ln_g, ln_b, w_s, b_s, *, tc):
    T = z.shape[0]
    width = SG_GROUPS * LANES
    ws = (w_s * jnp.tril(jnp.ones((CHUNK, CHUNK), w_s.dtype))).astype(BF16)
    bias = jnp.repeat(b_s.T, LANES, axis=1)
    return pl.pallas_call(
        _sgate_kernel,
        grid=(T // tc,),
        in_specs=[pl.BlockSpec((tc, width), lambda i: (i, 4)),
                  pl.BlockSpec((tc, width), lambda i: (i, 5)),
                  pl.BlockSpec((1, width), lambda i: (0, 0)),
                  pl.BlockSpec((1, width), lambda i: (0, 0)),
                  pl.BlockSpec((SG_GROUPS, CHUNK, CHUNK), lambda i: (0, 0, 0)),
                  pl.BlockSpec((CHUNK, width), lambda i: (0, 0))],
        out_specs=pl.BlockSpec((tc, width), lambda i: (i, 0)),
        out_shape=jax.ShapeDtypeStruct((T, width), BF16),
        compiler_params=_params("parallel"),
        name="chunk_spatial_gate",
    )(z, z, ln_g.reshape(1, width), ln_b.reshape(1, width), ws, bias)


def _merge_kernel(a_ref, p_ref, s_ref, gate_ref, x_ref, wb_ref, wo_ref, o_ref):
    d = x_ref.shape[1]
    merged = None
    for n, br in enumerate((a_ref, p_ref, s_ref)):
        proj = jnp.dot(br[...], wb_ref[n], preferred_element_type=F32)
        gate = jax.nn.sigmoid(gate_ref[:, n * d:(n + 1) * d].astype(F32))
        merged = gate * proj if merged is None else merged + gate * proj
    o_ref[...] = x_ref[...] + jnp.dot(merged.astype(BF16), wo_ref[...], preferred_element_type=F32)


def branch_merge(a_out, p_out, s_out, z, x, w_branch, w_out, *, tm):
    T, D = x.shape
    bw = a_out.shape[1]
    br_spec = pl.BlockSpec((tm, bw), lambda i: (i, 0))
    return pl.pallas_call(
        _merge_kernel,
        grid=(T // tm,),
        in_specs=[br_spec, br_spec, br_spec,
                  pl.BlockSpec((tm, N_BRANCH * D), lambda i: (i, 1)),
                  pl.BlockSpec((tm, D), lambda i: (i, 0)),
                  pl.BlockSpec((N_BRANCH, bw, D), lambda i: (0, 0, 0)),
                  pl.BlockSpec((D, D), lambda i: (0, 0))],
        out_specs=pl.BlockSpec((tm, D), lambda i: (i, 0)),
        out_shape=jax.ShapeDtypeStruct((T, D), F32),
        compiler_params=_params("parallel"),
        name="branch_merge",
    )(a_out, p_out, s_out, z, x, w_branch.astype(BF16), w_out.astype(BF16))


def _cross_kernel(x_ref, g_ref, k_ref, v_ref, wq_ref, wo_ref, o_ref):
    x = x_ref[...]
    d = x.shape[1]
    hd = d // X_HEADS
    xn = _rmsnorm_rows(x, g_ref[...]).astype(BF16)
    q = jnp.dot(xn, wq_ref[...], preferred_element_type=F32).astype(BF16)
    heads = []
    for h in range(X_HEADS):
        cols = slice(h * hd, (h + 1) * hd)
        s = lax.dot_general(q[:, cols], k_ref[:, cols], (((1,), (1,)), ((), ())), preferred_element_type=F32)
        s = s * (hd ** -0.5)
        p = jnp.exp(s - jnp.max(s, axis=1, keepdims=True))
        p = p / jnp.sum(p, axis=1, keepdims=True)
        heads.append(jnp.dot(p.astype(BF16), v_ref[:, cols], preferred_element_type=F32).astype(BF16))
    o = jnp.concatenate(heads, axis=1)
    o_ref[...] = x + jnp.dot(o, wo_ref[...], preferred_element_type=F32)


def cross_attention(x, g, kv, w_q, w_o, *, batch, seq, tm):
    T, D = x.shape
    M = kv.shape[0] // batch
    nb = seq // tm
    return pl.pallas_call(
        _cross_kernel,
        grid=(batch, nb),
        in_specs=[pl.BlockSpec((tm, D), lambda b, i: (b * nb + i, 0)),
                  pl.BlockSpec((1, D), lambda b, i: (0, 0)),
                  pl.BlockSpec((M, D), lambda b, i: (b, 0)),
                  pl.BlockSpec((M, D), lambda b, i: (b, 1)),
                  pl.BlockSpec((D, D), lambda b, i: (0, 0)),
                  pl.BlockSpec((D, D), lambda b, i: (0, 0))],
        out_specs=pl.BlockSpec((tm, D), lambda b, i: (b * nb + i, 0)),
        out_shape=jax.ShapeDtypeStruct((T, D), F32),
        compiler_params=_params("parallel", "parallel"),
        name="cross_attention",
    )(x, g.reshape(1, D), kv, kv, w_q.astype(BF16), w_o.astype(BF16))


def _peer_scores_kernel(x_ref, g_ref, wq_ref, keys_ref, o_ref):
    xn = _rmsnorm_rows(x_ref[...], g_ref[...]).astype(BF16)
    q = jnp.dot(xn, wq_ref[...], preferred_element_type=F32).astype(BF16)
    half = keys_ref.shape[2]
    for hp in range(keys_ref.shape[0]):
        o_ref[hp] = lax.dot_general(keys_ref[hp], q[:, hp * half:(hp + 1) * half],
                                    (((1,), (1,)), ((), ())), preferred_element_type=F32)


def peer_scores(x, g, w_pq, sub_keys, *, tm):
    T, D = x.shape
    wq = w_pq.reshape(D, -1).astype(BF16)
    keys = sub_keys.reshape(-1, sub_keys.shape[-2], sub_keys.shape[-1]).astype(BF16)
    nhp, nk, half = keys.shape
    return pl.pallas_call(
        _peer_scores_kernel,
        grid=(T // tm,),
        in_specs=[pl.BlockSpec((tm, D), lambda i: (i, 0)),
                  pl.BlockSpec((1, D), lambda i: (0, 0)),
                  pl.BlockSpec((D, nhp * half), lambda i: (0, 0)),
                  pl.BlockSpec((nhp, nk, half), lambda i: (0, 0, 0))],
        out_specs=pl.BlockSpec((nhp, nk, tm), lambda i: (0, 0, i)),
        out_shape=jax.ShapeDtypeStruct((nhp, nk, T), F32),
        compiler_params=_params("parallel"),
        name="peer_scores",
    )(x, g.reshape(1, D), wq, keys)


def _tree_reduce(op, xs):
    xs = [xs[i] for i in range(len(xs))]
    while len(xs) > 1:
        half = (len(xs) + 1) // 2
        xs = [op(xs[i], xs[i + half]) if i + half < len(xs) else xs[i] for i in range(half)]
    return xs[0]


def _peer_route_kernel(s_ref, rank_ref, e2_ref, cnt_ref, c1_ref, work_ref, top_ref, arg_ref):
    nk = s_ref.shape[1]
    shape = s_ref.shape[1:]
    key_iota = lax.broadcasted_iota(jnp.int32, shape, 0).astype(F32)

    for side in range(2):
        work_ref[...] = s_ref[side]

        def extract(r, _, side=side):
            s = work_ref[...]
            m = _tree_reduce(jnp.maximum, s)
            idx = _tree_reduce(jnp.minimum, jnp.where(s == m[None], key_iota, float(nk)))
            work_ref[...] = jnp.where(key_iota == idx[None], -jnp.inf, s)
            top_ref[side, r] = m
            arg_ref[side, r] = idx
            return 0

        lax.fori_loop(0, PEER_TOPK, extract, 0)

    t0 = [top_ref[0, a] for a in range(PEER_TOPK)]
    t1 = [top_ref[1, b] for b in range(PEER_TOPK)]
    cand = [t0[a] + t1[b] for a, b in PEER_CELLS]
    best = cand[0]
    live = list(cand)
    chosen = [None] * len(PEER_CELLS)
    for _ in range(PEER_TOPK):
        m = _tree_reduce(jnp.maximum, live)
        first = _tree_reduce(jnp.minimum, [jnp.where(live[ci] == m, float(ci), float(len(PEER_CELLS)))
                                           for ci in range(len(PEER_CELLS))])
        for ci in range(len(PEER_CELLS)):
            hit = first == float(ci)
            chosen[ci] = hit if chosen[ci] is None else chosen[ci] | hit
            live[ci] = jnp.where(hit, -jnp.inf, live[ci])
    denom = None
    cnt_a = [jnp.zeros(best.shape, F32) for _ in range(PEER_TOPK)]
    for ci, (a, b) in enumerate(PEER_CELLS):
        term = jnp.where(chosen[ci], jnp.exp(cand[ci] - best), 0.0)
        denom = term if denom is None else denom + term
        cnt_a[a] = cnt_a[a] + jnp.where(chosen[ci], 1.0, 0.0)
    inv_denom = 1.0 / denom

    cnt = jnp.zeros(shape, F32)
    rank = jnp.full(shape, float(PEER_TOPK), F32)
    for r in range(PEER_TOPK):
        cnt = jnp.where(key_iota == arg_ref[0, r][None], cnt_a[r][None], cnt)
        rank = jnp.where(key_iota == arg_ref[1, r][None], float(r), rank)
    cnt_ref[...] = cnt
    c1_ref[...] = jnp.exp(s_ref[0] - t0[0][None]) * (0.5 * inv_denom)[None]
    rank_ref[...] = rank
    e2_ref[...] = jnp.exp(s_ref[1] - t1[0][None])


def peer_route(scores):
    _, nk, nh, T = scores.shape
    tb = LANES
    spec = pl.BlockSpec((nk, nh, tb), lambda i: (0, 0, i))
    out = jax.ShapeDtypeStruct((nk, nh, T), F32)
    return pl.pallas_call(
        _peer_route_kernel,
        grid=(T // tb,),
        in_specs=[pl.BlockSpec((2, nk, nh, tb), lambda i: (0, 0, 0, i))],
        out_specs=[spec] * 4,
        out_shape=[out] * 4,
        scratch_shapes=[pltpu.VMEM((nk, nh, tb), F32),
                        pltpu.VMEM((2, PEER_TOPK, nh, tb), F32),
                        pltpu.VMEM((2, PEER_TOPK, nh, tb), F32)],
        compiler_params=_params("parallel"),
        name="peer_route",
    )(scores)


def _peer_dense_kernel(x_ref, g_ref, u_ref, vt_ref, rank_ref, e2_ref, cnt_ref, c1_ref, o_ref, xnt_ref, acc_ref):
    e = pl.program_id(1)

    @pl.when(e == 0)
    def _():
        xn = _rmsnorm_rows(x_ref[...], g_ref[...])
        xnt_ref[...] = xn.T.astype(BF16)
        acc_ref[...] = jnp.zeros_like(acc_ref)

    h = jnp.dot(u_ref[...], xnt_ref[...], preferred_element_type=F32)
    act = (h * (1.0 + lax.erf(h * (2.0 ** -0.5)))).astype(BF16)
    nh, groups, rows, tb = rank_ref.shape
    n_tiles = u_ref.shape[0] // (groups * rows)
    act = act.reshape(n_tiles, groups, rows, tb)
    zero = jnp.zeros((groups, rows, tb), BF16)
    w = [None] * n_tiles
    for hd in range(nh):
        rank = rank_ref[hd]
        e2 = e2_ref[hd]
        for ii in range(n_tiles):
            cnt = jnp.broadcast_to(cnt_ref[ii, hd:hd + 1, :], (rows, tb)).astype(BF16)
            c1 = jnp.broadcast_to(c1_ref[ii, hd:hd + 1, :], (rows, tb)).astype(BF16)
            term = jnp.where(rank < cnt[None], e2 * c1[None], zero)
            w[ii] = term if w[ii] is None else w[ii] + term
    a = jnp.stack([act[ii] * w[ii] for ii in range(n_tiles)], axis=0).reshape(u_ref.shape[0], tb)
    acc_ref[...] += jnp.dot(vt_ref[...], a, preferred_element_type=F32)

    @pl.when(e == pl.num_programs(1) - 1)
    def _():
        o_ref[...] = x_ref[...] + acc_ref[...].T


def peer_dense(x, g, u, vt, rank, e2, cnt, c1, *, tb, te):
    T, D = x.shape
    E = u.shape[0]
    nh, nk, _ = rank.shape
    rank = rank.reshape(nh, nk // BF16_ROWS, BF16_ROWS, T)
    e2 = e2.reshape(nh, nk // BF16_ROWS, BF16_ROWS, T)
    tab_spec = pl.BlockSpec((nh, nk // BF16_ROWS, BF16_ROWS, tb), lambda i, e: (0, 0, 0, i))
    row_spec = pl.BlockSpec((te // nk, nh, tb), lambda i, e: (e, 0, i))
    return pl.pallas_call(
        _peer_dense_kernel,
        grid=(T // tb, E // te),
        in_specs=[pl.BlockSpec((tb, D), lambda i, e: (i, 0)),
                  pl.BlockSpec((1, D), lambda i, e: (0, 0)),
                  pl.BlockSpec((te, D), lambda i, e: (e, 0)),
                  pl.BlockSpec((D, te), lambda i, e: (0, e)),
                  tab_spec, tab_spec, row_spec, row_spec],
        out_specs=pl.BlockSpec((tb, D), lambda i, e: (i, 0)),
        out_shape=jax.ShapeDtypeStruct((T, D), F32),
        scratch_shapes=[pltpu.VMEM((D, tb), BF16), pltpu.VMEM((D, tb), F32)],
        compiler_params=_params("parallel", "arbitrary"),
        name="peer_dense",
    )(x, g.reshape(1, D), u, vt, rank, e2, cnt, c1)


def _expert_tables_kernel(u_ref, v_ref, ub_ref, vt_ref):
    ub_ref[...] = u_ref[0].astype(ub_ref.dtype)
    vt_ref[...] = v_ref[0].T.astype(vt_ref.dtype)


def expert_tables(peer_u, peer_v, layer, *, te):
    _, E, D = peer_u.shape
    in_spec = pl.BlockSpec((1, te, D), lambda e: (layer, e, 0))
    return pl.pallas_call(
        _expert_tables_kernel,
        grid=(E // te,),
        in_specs=[in_spec, in_spec],
        out_specs=[pl.BlockSpec((te, D), lambda e: (e, 0)), pl.BlockSpec((D, te), lambda e: (0, e))],
        out_shape=[jax.ShapeDtypeStruct((E, D), BF16), jax.ShapeDtypeStruct((D, E), BF16)],
        compiler_params=_params("parallel"),
        name="expert_tables",
    )(peer_u, peer_v)


def _final_norm_kernel(x_ref, g_ref, o_ref):
    o_ref[...] = _rmsnorm_rows(x_ref[...], g_ref[...])


def final_norm(x, g, *, tm):
    T, D = x.shape
    return pl.pallas_call(
        _final_norm_kernel,
        grid=(T // tm,),
        in_specs=[pl.BlockSpec((tm, D), lambda i: (i, 0)), pl.BlockSpec((1, D), lambda i: (0, 0))],
        out_specs=pl.BlockSpec((tm, D), lambda i: (i, 0)),
        out_shape=jax.ShapeDtypeStruct((T, D), F32),
        compiler_params=_params("parallel"),
        name="final_norm",
    )(x, g.reshape(1, D))


def _tile(n, want):
    if n <= want:
        return n
    t = want - want % LANES
    while n % t:
        t -= LANES
    return t


def kernel(x, mem, positions, g_mix, w_in, lam_qk, subln_g, w_pool, pool_scale, sg_ln_g, sg_ln_b, w_spatial,
           b_spatial, w_branch, w_out, g_cross, g_mem, w_xq, w_xkv, w_xo, g_ffn, w_pq, sub_keys, peer_u, peer_v,
           g_final):
    B, S, D = x.shape
    T = B * S
    depth = w_in.shape[0]
    n_keys = sub_keys.shape[3]
    x = x.reshape(T, D)
    memf = mem.reshape(-1, D)
    cos_tab, sin_tab, perm = rope_tables(positions)

    tm = _tile(S, 1024)
    tb_dense = _tile(S, 512)

    for l in range(depth):
        lam_init = 0.8 - 0.6 * math.exp(-0.3 * l)
        lq = lam_qk[l].astype(F32)
        lam = (jnp.exp(jnp.sum(lq[0] * lq[1])) - jnp.exp(jnp.sum(lq[2] * lq[3])) + lam_init).reshape(1)

        z = norm_matmul(x, g_mix[l], w_in[l].astype(BF16), tm=tm, tn=1536, name="in_proj")
        qk = rope_qk(z, cos_tab, sin_tab, perm, tm=tm)
        a_out = diff_attention(qk, z, lam, subln_g[l], batch=B, seq=S, tq=_tile(S, 512), tk=_tile(S, 512),
                               lam_init=lam_init)
        p_out = multiscale_pool(z, w_pool[l], pool_scale[l], batch=B, seq=S, tc=_tile(S, 256))
        s_out = chunk_spatial_gate(z, sg_ln_g[l], sg_ln_b[l], w_spatial[l], b_spatial[l], tc=_tile(S, 512))
        x = branch_merge(a_out, p_out, s_out, z, x, w_branch[l], w_out[l], tm=_tile(S, 512))

        kv = norm_matmul(memf, g_mem[l], w_xkv[l].astype(BF16), tm=_tile(memf.shape[0], 1024), tn=1024,
                         name="mem_kv")
        x = cross_attention(x, g_cross[l], kv, w_xq[l], w_xo[l], batch=B, seq=S, tm=_tile(S, 512))

        scores = peer_scores(x, g_ffn[l], w_pq[l], sub_keys[l], tm=_tile(S, 512))
        scores = scores.reshape(PEER_HEADS, 2, n_keys, T).transpose(1, 2, 0, 3)
        rank, e2, cnt, c1 = peer_route(scores)
        u_bf, vt_bf = expert_tables(peer_u, peer_v, l, te=512)
        x = peer_dense(x, g_ffn[l], u_bf, vt_bf,
                       rank.transpose(1, 0, 2).astype(BF16), e2.transpose(1, 0, 2).astype(BF16), cnt, c1,
                       tb=tb_dense, te=2048)

    return final_norm(x, g_final, tm=tm).reshape(B, S, D)
```

```python
import functools
import math

import jax
import jax.numpy as jnp
from jax import lax
from jax.experimental import pallas as pl
from jax.experimental.pallas import tpu as pltpu

F32 = jnp.float32
BF16 = jnp.bfloat16

EPS = 1e-6
LANES = 128
BF16_ROWS = 16
VMEM_LIMIT_BYTES = 48 * 1024 * 1024

DA_HEADS = 4
DA_HEAD_DIM = 64
ROT_DIM = DA_HEAD_DIM // 4
ROPE_THETA = 500000.0
POOL_WINDOWS = (2, 4, 8, 16)
CHUNK = 128
SG_GROUPS = 4
N_BRANCH = 3
X_HEADS = 4
PEER_TOPK = 16
PEER_CELLS = tuple((a, b) for a in range(PEER_TOPK) for b in range(PEER_TOPK) if (a + 1) * (b + 1) <= PEER_TOPK)


def _params(*sem):
    return pltpu.CompilerParams(dimension_semantics=sem, vmem_limit_bytes=VMEM_LIMIT_BYTES)


def _rmsnorm_rows(x, g):
    ms = jnp.mean(x * x, axis=-1, keepdims=True)
    return x * lax.rsqrt(ms + EPS) * g


def _norm_matmul_kernel(x_ref, g_ref, w_ref, o_ref, xn_ref):
    @pl.when(pl.program_id(1) == 0)
    def _():
        xn_ref[...] = _rmsnorm_rows(x_ref[...], g_ref[...]).astype(BF16)

    o_ref[...] = jnp.dot(xn_ref[...], w_ref[...], preferred_element_type=F32).astype(o_ref.dtype)


def norm_matmul(x, g, w, *, tm, tn, name):
    T, D = x.shape
    N = w.shape[1]
    return pl.pallas_call(
        _norm_matmul_kernel,
        grid=(T // tm, N // tn),
        in_specs=[pl.BlockSpec((tm, D), lambda i, j: (i, 0)),
                  pl.BlockSpec((1, D), lambda i, j: (0, 0)),
                  pl.BlockSpec((D, tn), lambda i, j: (0, j))],
        out_specs=pl.BlockSpec((tm, tn), lambda i, j: (i, j)),
        out_shape=jax.ShapeDtypeStruct((T, N), BF16),
        scratch_shapes=[pltpu.VMEM((tm, D), BF16)],
        compiler_params=_params("parallel", "arbitrary"),
        name=name,
    )(x, g.reshape(1, D), w)


def _rope_kernel(z_ref, c_ref, s_ref, p_ref, o_ref, *, n_q_groups):
    c = c_ref[...]
    s = s_ref[...]
    for gi in range(z_ref.shape[1] // LANES):
        cols = slice(gi * LANES, (gi + 1) * LANES)
        xg = z_ref[:, cols]
        swapped = jnp.dot(xg, p_ref[...], preferred_element_type=F32)
        r = xg.astype(F32) * c + swapped * s
        if gi < n_q_groups:
            r = r * (DA_HEAD_DIM ** -0.5)
        o_ref[:, cols] = r.astype(o_ref.dtype)


def rope_qk(z, cos_tab, sin_tab, perm, *, tm):
    T = z.shape[0]
    width = 2 * DA_HEADS * 2 * DA_HEAD_DIM
    return pl.pallas_call(
        functools.partial(_rope_kernel, n_q_groups=width // (2 * LANES)),
        grid=(T // tm,),
        in_specs=[pl.BlockSpec((tm, width), lambda i: (i, 0)),
                  pl.BlockSpec((tm, LANES), lambda i: (i, 0)),
                  pl.BlockSpec((tm, LANES), lambda i: (i, 0)),
                  pl.BlockSpec((LANES, LANES), lambda i: (0, 0))],
        out_specs=pl.BlockSpec((tm, width), lambda i: (i, 0)),
        out_shape=jax.ShapeDtypeStruct((T, width), BF16),
        compiler_params=_params("parallel"),
        name="rope_qk",
    )(z, cos_tab, sin_tab, perm)


def rope_tables(positions):
    half = ROT_DIM // 2
    inv = ROPE_THETA ** (-jnp.arange(0, ROT_DIM, 2, dtype=F32) / ROT_DIM)
    ang = positions.astype(F32).reshape(-1, 1) * inv
    cos, sin = jnp.cos(ang), jnp.sin(ang)
    T = ang.shape[0]
    ones = jnp.ones((T, DA_HEAD_DIM - ROT_DIM), F32)
    zeros = jnp.zeros((T, DA_HEAD_DIM - ROT_DIM), F32)
    c64 = jnp.concatenate([cos, cos, ones], axis=1)
    s64 = jnp.concatenate([-sin, sin, zeros], axis=1)
    lane = jnp.arange(LANES)
    d = lane % DA_HEAD_DIM
    src = jnp.where(d < half, lane + half, jnp.where(d < ROT_DIM, lane - half, -1))
    perm = (lane[:, None] == src[None, :]).astype(BF16)
    return jnp.tile(c64, (1, 2)), jnp.tile(s64, (1, 2)), perm


def _diff_attn_kernel(lam_ref, q_ref, k_ref, v_ref, g_ref, o_ref, *, tq, tk, lam_init):
    qi = pl.program_id(2)
    q = q_ref[...]
    lane = lax.broadcasted_iota(jnp.int32, q.shape, 1)
    zero = jnp.zeros_like(q)
    q_maps = (jnp.where(lane < DA_HEAD_DIM, q, zero), jnp.where(lane >= DA_HEAD_DIM, q, zero))
    row = lax.broadcasted_iota(jnp.int32, (tq, tk), 0)
    col = lax.broadcasted_iota(jnp.int32, (tq, tk), 1)
    per_q = tq // tk

    def block(kb, carry, diag_offset):
        start = pl.multiple_of(kb * tk, tk)
        k = k_ref[pl.ds(start, tk), :]
        v = v_ref[pl.ds(start, tk), :]
        new = []
        for mi in range(2):
            m, l, acc = carry[3 * mi:3 * mi + 3]
            s = lax.dot_general(q_maps[mi], k, (((1,), (1,)), ((), ())), preferred_element_type=F32)
            if diag_offset is not None:
                s = jnp.where(col + diag_offset <= row, s, -jnp.inf)
            m_new = jnp.maximum(m, jnp.max(s, axis=1, keepdims=True))
            p = jnp.exp(s - m_new)
            alpha = jnp.exp(m - m_new)
            l = alpha * l + jnp.sum(p, axis=1, keepdims=True)
            acc = alpha * acc + jnp.dot(p.astype(BF16), v, preferred_element_type=F32)
            new += [m_new, l, acc]
        return tuple(new)

    init = (jnp.full((tq, 1), -jnp.inf, F32), jnp.zeros((tq, 1), F32), jnp.zeros((tq, LANES), F32)) * 2
    carry = lax.fori_loop(0, qi * per_q, lambda kb, c: block(kb, c, None), init)
    for d in range(per_q):
        carry = block(qi * per_q + d, carry, d * tk)
    m1, l1, a1, m2, l2, a2 = carry
    o = a1 / l1 - lam_ref[0] * (a2 / l2)
    o_ref[...] = (_rmsnorm_rows(o, g_ref[...]) * (1.0 - lam_init)).astype(o_ref.dtype)


def diff_attention(qk, z, lam, subln_g, *, batch, seq, tq, tk, lam_init):
    T = qk.shape[0]
    nq = seq // tq
    width = DA_HEADS * 2 * DA_HEAD_DIM
    return pl.pallas_call(
        functools.partial(_diff_attn_kernel, tq=tq, tk=tk, lam_init=lam_init),
        grid=(batch, DA_HEADS, nq),
        in_specs=[pl.BlockSpec(memory_space=pltpu.SMEM),
                  pl.BlockSpec((tq, LANES), lambda b, h, i: (b * nq + i, h)),
                  pl.BlockSpec((seq, LANES), lambda b, h, i: (b, DA_HEADS + h)),
                  pl.BlockSpec((seq, LANES), lambda b, h, i: (b, 2 * DA_HEADS + h)),
                  pl.BlockSpec((1, LANES), lambda b, h, i: (0, 0))],
        out_specs=pl.BlockSpec((tq, LANES), lambda b, h, i: (b * nq + i, h)),
        out_shape=jax.ShapeDtypeStruct((T, width), BF16),
        compiler_params=_params("parallel", "parallel", "arbitrary"),
        name="diff_attention",
    )(lam, qk, qk, z, subln_g.reshape(1, LANES))


def _pool_kernel(p_ref, prev_ref, cur_band_ref, prev_band_ref, inv_ref, w_ref, scale_ref, o_ref):
    has_prev = pl.program_id(1) > 0
    for g in range(len(POOL_WINDOWS)):
        cols = slice(g * LANES, (g + 1) * LANES)
        pg = p_ref[:, cols]
        wsum = jnp.dot(cur_band_ref[g], pg, preferred_element_type=F32)
        halo = jnp.dot(prev_band_ref[g], prev_ref[:, cols], preferred_element_type=F32)
        wsum = wsum + jnp.where(has_prev, halo, 0.0)
        d = wsum * inv_ref[:, cols] - pg.astype(F32)
        y = jnp.dot(d.astype(BF16), w_ref[g], preferred_element_type=F32) * scale_ref[:, cols]
        o_ref[:, cols] = y.astype(o_ref.dtype)


def multiscale_pool(z, w_pool, pool_scale, *, batch, seq, tc):
    T = z.shape[0]
    nc = seq // tc
    width = len(POOL_WINDOWS) * LANES
    col_block = 3
    t = jnp.arange(tc)
    delta = t[:, None] - t[None, :]
    win = jnp.asarray(POOL_WINDOWS)[:, None, None]
    cur_band = ((delta >= 0) & (delta < win)).astype(BF16)
    prev_band = ((delta + tc >= 0) & (delta + tc < win)).astype(BF16)
    pos = jnp.arange(seq)
    inv_cnt = 1.0 / jnp.minimum(pos[:, None] + 1, jnp.repeat(jnp.asarray(POOL_WINDOWS), LANES)[None, :]).astype(F32)
    return pl.pallas_call(
        _pool_kernel,
        grid=(batch, nc),
        in_specs=[pl.BlockSpec((tc, width), lambda b, c: (b * nc + c, col_block)),
                  pl.BlockSpec((tc, width), lambda b, c: (b * nc + jnp.maximum(c - 1, 0), col_block)),
                  pl.BlockSpec((len(POOL_WINDOWS), tc, tc), lambda b, c: (0, 0, 0)),
                  pl.BlockSpec((len(POOL_WINDOWS), tc, tc), lambda b, c: (0, 0, 0)),
                  pl.BlockSpec((tc, width), lambda b, c: (c, 0)),
                  pl.BlockSpec((len(POOL_WINDOWS), LANES, LANES), lambda b, c: (0, 0, 0)),
                  pl.BlockSpec((1, width), lambda b, c: (0, 0))],
        out_specs=pl.BlockSpec((tc, width), lambda b, c: (b * nc + c, 0)),
        out_shape=jax.ShapeDtypeStruct((T, width), BF16),
        compiler_params=_params("parallel", "arbitrary"),
        name="multiscale_pool",
    )(z, z, cur_band, prev_band, inv_cnt, w_pool.astype(BF16), pool_scale.reshape(1, width))


def _sgate_kernel(u_ref, v_ref, g_ref, b_ref, ws_ref, bias_ref, o_ref):
    v = v_ref[...].astype(F32)
    mu = jnp.mean(v, axis=-1, keepdims=True)
    vc = v - mu
    var = jnp.mean(vc * vc, axis=-1, keepdims=True)
    vn = (vc * lax.rsqrt(var + EPS) * g_ref[...] + b_ref[...]).astype(BF16)
    for n in range(v.shape[0] // CHUNK):
        rows = slice(n * CHUNK, (n + 1) * CHUNK)
        for g in range(SG_GROUPS):
            cols = slice(g * LANES, (g + 1) * LANES)
            mixed = jnp.dot(ws_ref[g], vn[rows, cols], preferred_element_type=F32) + bias_ref[:, cols]
            o_ref[rows, cols] = (u_ref[rows, cols].astype(F32) * mixed).astype(o_ref.dtype)


def chunk_spatial_gate(z, ln_g, ln_b, w_s, b_s, *, tc):
    T = z.shape[0]
    width = SG_GROUPS * LANES
    ws = (w_s * jnp.tril(jnp.ones((CHUNK, CHUNK), w_s.dtype))).astype(BF16)
    bias = jnp.repeat(b_s.T, LANES, axis=1)
    return pl.pallas_call(
        _sgate_kernel,
        grid=(T // tc,),
        in_specs=[pl.BlockSpec((tc, width), lambda i: (i, 4)),
                  pl.BlockSpec((tc, width), lambda i: (i, 5)),
                  pl.BlockSpec((1, width), lambda i: (0, 0)),
                  pl.BlockSpec((1, width), lambda i: (0, 0)),
                  pl.BlockSpec((SG_GROUPS, CHUNK, CHUNK), lambda i: (0, 0, 0)),
                  pl.BlockSpec((CHUNK, width), lambda i: (0, 0))],
        out_specs=pl.BlockSpec((tc, width), lambda i: (i, 0)),
        out_shape=jax.ShapeDtypeStruct((T, width), BF16),
        compiler_params=_params("parallel"),
        name="chunk_spatial_gate",
    )(z, z, ln_g.reshape(1, width), ln_b.reshape(1, width), ws, bias)


def _merge_kernel(a_ref, p_ref, s_ref, gate_ref, x_ref, wb_ref, wo_ref, o_ref):
    d = x_ref.shape[1]
    merged = None
    for n, br in enumerate((a_ref, p_ref, s_ref)):
        proj = jnp.dot(br[...], wb_ref[n], preferred_element_type=F32)
        gate = jax.nn.sigmoid(gate_ref[:, n * d:(n + 1) * d].astype(F32))
        merged = gate * proj if merged is None else merged + gate * proj
    o_ref[...] = x_ref[...] + jnp.dot(merged.astype(BF16), wo_ref[...], preferred_element_type=F32)


def branch_merge(a_out, p_out, s_out, z, x, w_branch, w_out, *, tm):
    T, D = x.shape
    bw = a_out.shape[1]
    br_spec = pl.BlockSpec((tm, bw), lambda i: (i, 0))
    return pl.pallas_call(
        _merge_kernel,
        grid=(T // tm,),
        in_specs=[br_spec, br_spec, br_spec,
                  pl.BlockSpec((tm, N_BRANCH * D), lambda i: (i, 1)),
                  pl.BlockSpec((tm, D), lambda i: (i, 0)),
                  pl.BlockSpec((N_BRANCH, bw, D), lambda i: (0, 0, 0)),
                  pl.BlockSpec((D, D), lambda i: (0, 0))],
        out_specs=pl.BlockSpec((tm, D), lambda i: (i, 0)),
        out_shape=jax.ShapeDtypeStruct((T, D), F32),
        compiler_params=_params("parallel"),
        name="branch_merge",
    )(a_out, p_out, s_out, z, x, w_branch.astype(BF16), w_out.astype(BF16))


def _cross_kernel(x_ref, g_ref, k_ref, v_ref, wq_ref, wo_ref, o_ref):
    x = x_ref[...]
    d = x.shape[1]
    hd = d // X_HEADS
    xn = _rmsnorm_rows(x, g_ref[...]).astype(BF16)
    q = jnp.dot(xn, wq_ref[...], preferred_element_type=F32).astype(BF16)
    heads = []
    for h in range(X_HEADS):
        cols = slice(h * hd, (h + 1) * hd)
        s = lax.dot_general(q[:, cols], k_ref[:, cols], (((1,), (1,)), ((), ())), preferred_element_type=F32)
        s = s * (hd ** -0.5)
        p = jnp.exp(s - jnp.max(s, axis=1, keepdims=True))
        p = p / jnp.sum(p, axis=1, keepdims=True)
        heads.append(jnp.dot(p.astype(BF16), v_ref[:, cols], preferred_element_type=F32).astype(BF16))
    o = jnp.concatenate(heads, axis=1)
    o_ref[...] = x + jnp.dot(o, wo_ref[...], preferred_element_type=F32)


def cross_attention(x, g, kv, w_q, w_o, *, batch, seq, tm):
    T, D = x.shape
    M = kv.shape[0] // batch
    nb = seq // tm
    return pl.pallas_call(
        _cross_kernel,
        grid=(batch, nb),
        in_specs=[pl.BlockSpec((tm, D), lambda b, i: (b * nb + i, 0)),
                  pl.BlockSpec((1, D), lambda b, i: (0, 0)),
                  pl.BlockSpec((M, D), lambda b, i: (b, 0)),
                  pl.BlockSpec((M, D), lambda b, i: (b, 1)),
                  pl.BlockSpec((D, D), lambda b, i: (0, 0)),
                  pl.BlockSpec((D, D), lambda b, i: (0, 0))],
        out_specs=pl.BlockSpec((tm, D), lambda b, i: (b * nb + i, 0)),
        out_shape=jax.ShapeDtypeStruct((T, D), F32),
        compiler_params=_params("parallel", "parallel"),
        name="cross_attention",
    )(x, g.reshape(1, D), kv, kv, w_q.astype(BF16), w_o.astype(BF16))


def _peer_scores_kernel(x_ref, g_ref, wq_ref, keys_ref, o_ref):
    xn = _rmsnorm_rows(x_ref[...], g_ref[...]).astype(BF16)
    q = jnp.dot(xn, wq_ref[...], preferred_element_type=F32).astype(BF16)
    _, nk, nh, tm = o_ref.shape
    width = keys_ref.shape[2]
    for side in range(2):
        st = lax.dot_general(keys_ref[side], q[:, side * width:(side + 1) * width],
                             (((1,), (1,)), ((), ())), preferred_element_type=F32)
        o_ref[side] = st.reshape(nk, nh, tm)


def peer_scores(x, g, w_pq, sub_keys, *, tm):
    T, D = x.shape
    nh, _, nk, half = sub_keys.shape
    wq = w_pq.reshape(D, nh, 2, half).transpose(0, 2, 1, 3).reshape(D, 2 * nh * half).astype(BF16)
    keys = jnp.einsum('hsnk,hg->snhgk', sub_keys, jnp.eye(nh, dtype=sub_keys.dtype))
    keys = keys.reshape(2, nk * nh, nh * half).astype(BF16)
    return pl.pallas_call(
        _peer_scores_kernel,
        grid=(T // tm,),
        in_specs=[pl.BlockSpec((tm, D), lambda i: (i, 0)),
                  pl.BlockSpec((1, D), lambda i: (0, 0)),
                  pl.BlockSpec((D, 2 * nh * half), lambda i: (0, 0)),
                  pl.BlockSpec((2, nk * nh, nh * half), lambda i: (0, 0, 0))],
        out_specs=pl.BlockSpec((2, nk, nh, tm), lambda i: (0, 0, 0, i)),
        out_shape=jax.ShapeDtypeStruct((2, nk, nh, T), F32),
        compiler_params=_params("parallel"),
        name="peer_scores",
    )(x, g.reshape(1, D), wq, keys)


def _first_argmax(xs):
    nodes = [(xs[i], float(i)) for i in range(len(xs))]
    while len(nodes) > 1:
        merged = []
        for j in range(0, len(nodes) - 1, 2):
            (va, ia), (vb, ib) = nodes[j], nodes[j + 1]
            merged.append((jnp.maximum(va, vb), jnp.where(va >= vb, ia, ib)))
        if len(nodes) % 2:
            merged.append(nodes[-1])
        nodes = merged
    return nodes[0]


def _peer_route_kernel(s_ref, rank_ref, e2_ref, cnt_ref, c1_ref, work_ref, top_ref, arg_ref):
    nk = s_ref.shape[1]
    shape = s_ref.shape[1:]
    key_iota = lax.broadcasted_iota(jnp.int32, shape, 0).astype(F32)

    for side in range(2):
        work_ref[...] = s_ref[side]

        def extract(r, _, side=side):
            s = work_ref[...]
            m, idx = _first_argmax(s)
            work_ref[...] = jnp.where(key_iota == idx[None], -jnp.inf, s)
            top_ref[side, r] = m
            arg_ref[side, r] = idx
            return 0

        lax.fori_loop(0, PEER_TOPK, extract, 0)

    t0 = [top_ref[0, a] for a in range(PEER_TOPK)]
    t1 = [top_ref[1, b] for b in range(PEER_TOPK)]
    cand = [t0[a] + t1[b] for a, b in PEER_CELLS]
    best = cand[0]
    live = list(cand)
    chosen = [None] * len(PEER_CELLS)
    for _ in range(PEER_TOPK):
        _, first = _first_argmax(live)
        for ci in range(len(PEER_CELLS)):
            hit = first == float(ci)
            chosen[ci] = hit if chosen[ci] is None else chosen[ci] | hit
            live[ci] = jnp.where(hit, -jnp.inf, live[ci])
    denom = None
    cnt_a = [jnp.zeros(best.shape, F32) for _ in range(PEER_TOPK)]
    for ci, (a, b) in enumerate(PEER_CELLS):
        term = jnp.where(chosen[ci], jnp.exp(cand[ci] - best), 0.0)
        denom = term if denom is None else denom + term
        cnt_a[a] = cnt_a[a] + jnp.where(chosen[ci], 1.0, 0.0)
    inv_denom = 1.0 / denom

    cnt = jnp.zeros(shape, F32)
    rank = jnp.full(shape, float(PEER_TOPK), F32)
    for r in range(PEER_TOPK):
        cnt = jnp.where(key_iota == arg_ref[0, r][None], cnt_a[r][None], cnt)
        rank = jnp.where(key_iota == arg_ref[1, r][None], float(r), rank)
    cnt_ref[...] = cnt
    c1_ref[...] = jnp.exp(s_ref[0] - t0[0][None]) * (0.5 * inv_denom)[None]
    rank_ref[...] = rank
    e2_ref[...] = jnp.exp(s_ref[1] - t1[0][None])


def peer_route(scores):
    _, nk, nh, T = scores.shape
    tb = LANES
    spec = pl.BlockSpec((nk, nh, tb), lambda i: (0, 0, i))
    out = jax.ShapeDtypeStruct((nk, nh, T), F32)
    return pl.pallas_call(
        _peer_route_kernel,
        grid=(T // tb,),
        in_specs=[pl.BlockSpec((2, nk, nh, tb), lambda i: (0, 0, 0, i))],
        out_specs=[spec] * 4,
        out_shape=[out] * 4,
        scratch_shapes=[pltpu.VMEM((nk, nh, tb), F32),
                        pltpu.VMEM((2, PEER_TOPK, nh, tb), F32),
                        pltpu.VMEM((2, PEER_TOPK, nh, tb), F32)],
        compiler_params=_params("parallel"),
        name="peer_route",
    )(scores)


def _peer_dense_kernel(x_ref, g_ref, u_ref, vt_ref, rank_ref, e2_ref, cnt_ref, c1_ref, o_ref, xnt_ref, acc_ref, *,
                       chunk):
    e = pl.program_id(1)

    @pl.when(e == 0)
    def _():
        xn = _rmsnorm_rows(x_ref[...], g_ref[...])
        xnt_ref[...] = xn.T.astype(BF16)
        acc_ref[...] = jnp.zeros_like(acc_ref)

    nh, groups, rows, tb = rank_ref.shape
    nk = groups * rows
    zero = jnp.zeros((groups, rows, tb), BF16)
    xnt = xnt_ref[...]
    total = None
    for c in range(u_ref.shape[0] // chunk):
        h = jnp.dot(u_ref[c * chunk:(c + 1) * chunk, :], xnt, preferred_element_type=F32)
        act = (h * (1.0 + lax.erf(h * (2.0 ** -0.5)))).astype(BF16)
        n_tiles = chunk // nk
        act = act.reshape(n_tiles, groups, rows, tb)
        w = [None] * n_tiles
        for hd in range(nh):
            rank = rank_ref[hd]
            e2 = e2_ref[hd]
            for ii in range(n_tiles):
                tile = c * n_tiles + ii
                cnt = jnp.broadcast_to(cnt_ref[tile, hd:hd + 1, :], (rows, tb)).astype(BF16)
                c1 = jnp.broadcast_to(c1_ref[tile, hd:hd + 1, :], (rows, tb)).astype(BF16)
                term = jnp.where(rank < cnt[None], e2 * c1[None], zero)
                w[ii] = term if w[ii] is None else w[ii] + term
        a = jnp.stack([act[ii] * w[ii] for ii in range(n_tiles)], axis=0).reshape(chunk, tb)
        part = jnp.dot(vt_ref[:, c * chunk:(c + 1) * chunk], a, preferred_element_type=F32)
        total = part if total is None else total + part
    acc_ref[...] += total

    @pl.when(e == pl.num_programs(1) - 1)
    def _():
        o_ref[...] = x_ref[...] + acc_ref[...].T


def peer_dense(x, g, u, vt, rank, e2, cnt, c1, *, tb, te, chunk):
    T, D = x.shape
    E = u.shape[0]
    nh, nk, _ = rank.shape
    rank = rank.reshape(nh, nk // BF16_ROWS, BF16_ROWS, T)
    e2 = e2.reshape(nh, nk // BF16_ROWS, BF16_ROWS, T)
    tab_spec = pl.BlockSpec((nh, nk // BF16_ROWS, BF16_ROWS, tb), lambda i, e: (0, 0, 0, i))
    row_spec = pl.BlockSpec((te // nk, nh, tb), lambda i, e: (e, 0, i))
    return pl.pallas_call(
        functools.partial(_peer_dense_kernel, chunk=chunk),
        grid=(T // tb, E // te),
        in_specs=[pl.BlockSpec((tb, D), lambda i, e: (i, 0)),
                  pl.BlockSpec((1, D), lambda i, e: (0, 0)),
                  pl.BlockSpec((te, D), lambda i, e: (e, 0)),
                  pl.BlockSpec((D, te), lambda i, e: (0, e)),
                  tab_spec, tab_spec, row_spec, row_spec],
        out_specs=pl.BlockSpec((tb, D), lambda i, e: (i, 0)),
        out_shape=jax.ShapeDtypeStruct((T, D), F32),
        scratch_shapes=[pltpu.VMEM((D, tb), BF16), pltpu.VMEM((D, tb), F32)],
        compiler_params=_params("parallel", "arbitrary"),
        name="peer_dense",
    )(x, g.reshape(1, D), u, vt, rank, e2, cnt, c1)


def _expert_tables_kernel(u_ref, v_ref, ub_ref, vt_ref):
    ub_ref[...] = u_ref[0].astype(ub_ref.dtype)
    vt_ref[...] = v_ref[0].T.astype(vt_ref.dtype)


def expert_tables(peer_u, peer_v, layer, *, te):
    _, E, D = peer_u.shape
    in_spec = pl.BlockSpec((1, te, D), lambda e: (layer, e, 0))
    return pl.pallas_call(
        _expert_tables_kernel,
        grid=(E // te,),
        in_specs=[in_spec, in_spec],
        out_specs=[pl.BlockSpec((te, D), lambda e: (e, 0)), pl.BlockSpec((D, te), lambda e: (0, e))],
        out_shape=[jax.ShapeDtypeStruct((E, D), BF16), jax.ShapeDtypeStruct((D, E), BF16)],
        compiler_params=_params("parallel"),
        name="expert_tables",
    )(peer_u, peer_v)


def _final_norm_kernel(x_ref, g_ref, o_ref):
    o_ref[...] = _rmsnorm_rows(x_ref[...], g_ref[...])


def final_norm(x, g, *, tm):
    T, D = x.shape
    return pl.pallas_call(
        _final_norm_kernel,
        grid=(T // tm,),
        in_specs=[pl.BlockSpec((tm, D), lambda i: (i, 0)), pl.BlockSpec((1, D), lambda i: (0, 0))],
        out_specs=pl.BlockSpec((tm, D), lambda i: (i, 0)),
        out_shape=jax.ShapeDtypeStruct((T, D), F32),
        compiler_params=_params("parallel"),
        name="final_norm",
    )(x, g.reshape(1, D))


def _tile(n, want):
    if n <= want:
        return n
    t = want - want % LANES
    while n % t:
        t -= LANES
    return t


def kernel(x, mem, positions, g_mix, w_in, lam_qk, subln_g, w_pool, pool_scale, sg_ln_g, sg_ln_b, w_spatial,
           b_spatial, w_branch, w_out, g_cross, g_mem, w_xq, w_xkv, w_xo, g_ffn, w_pq, sub_keys, peer_u, peer_v,
           g_final):
    B, S, D = x.shape
    T = B * S
    depth = w_in.shape[0]
    x = x.reshape(T, D)
    memf = mem.reshape(-1, D)
    cos_tab, sin_tab, perm = rope_tables(positions)

    tm = _tile(S, 1024)
    tb_dense = _tile(S, 512)

    for l in range(depth):
        lam_init = 0.8 - 0.6 * math.exp(-0.3 * l)
        lq = lam_qk[l].astype(F32)
        lam = (jnp.exp(jnp.sum(lq[0] * lq[1])) - jnp.exp(jnp.sum(lq[2] * lq[3])) + lam_init).reshape(1)

        z = norm_matmul(x, g_mix[l], w_in[l].astype(BF16), tm=tm, tn=1536, name="in_proj")
        qk = rope_qk(z, cos_tab, sin_tab, perm, tm=tm)
        a_out = diff_attention(qk, z, lam, subln_g[l], batch=B, seq=S, tq=_tile(S, 1024), tk=_tile(S, 512),
                               lam_init=lam_init)
        p_out = multiscale_pool(z, w_pool[l], pool_scale[l], batch=B, seq=S, tc=_tile(S, 256))
        s_out = chunk_spatial_gate(z, sg_ln_g[l], sg_ln_b[l], w_spatial[l], b_spatial[l], tc=_tile(S, 512))
        x = branch_merge(a_out, p_out, s_out, z, x, w_branch[l], w_out[l], tm=_tile(S, 512))

        kv = norm_matmul(memf, g_mem[l], w_xkv[l].astype(BF16), tm=_tile(memf.shape[0], 1024), tn=1024,
                         name="mem_kv")
        x = cross_attention(x, g_cross[l], kv, w_xq[l], w_xo[l], batch=B, seq=S, tm=_tile(S, 512))

        scores = peer_scores(x, g_ffn[l], w_pq[l], sub_keys[l], tm=_tile(S, 512))
        rank, e2, cnt, c1 = peer_route(scores)
        u_bf, vt_bf = expert_tables(peer_u, peer_v, l, te=512)
        x = peer_dense(x, g_ffn[l], u_bf, vt_bf,
                       rank.transpose(1, 0, 2).astype(BF16), e2.transpose(1, 0, 2).astype(BF16), cnt, c1,
                       tb=tb_dense, te=2048, chunk=2048)

    return final_norm(x, g_final, tm=tm).reshape(B, S, D)
```

```python
import functools
import math

import jax
import jax.numpy as jnp
from jax import lax
from jax.experimental import pallas as pl
from jax.experimental.pallas import tpu as pltpu

F32 = jnp.float32
BF16 = jnp.bfloat16

EPS = 1e-6
LANES = 128
BF16_ROWS = 16
VMEM_LIMIT_BYTES = 48 * 1024 * 1024

DA_HEADS = 4
DA_HEAD_DIM = 64
ROT_DIM = DA_HEAD_DIM // 4
ROPE_THETA = 500000.0
POOL_WINDOWS = (2, 4, 8, 16)
CHUNK = 128
SG_GROUPS = 4
N_BRANCH = 3
X_HEADS = 4
PEER_TOPK = 16
PEER_CELLS = tuple((a, b) for a in range(PEER_TOPK) for b in range(PEER_TOPK) if (a + 1) * (b + 1) <= PEER_TOPK)


def _params(*sem):
    return pltpu.CompilerParams(dimension_semantics=sem, vmem_limit_bytes=VMEM_LIMIT_BYTES)


def _rmsnorm_rows(x, g):
    ms = jnp.mean(x * x, axis=-1, keepdims=True)
    return x * lax.rsqrt(ms + EPS) * g


def _norm_matmul_kernel(x_ref, g_ref, w_ref, o_ref, xn_ref):
    @pl.when(pl.program_id(1) == 0)
    def _():
        xn_ref[...] = _rmsnorm_rows(x_ref[...], g_ref[...]).astype(BF16)

    o_ref[...] = jnp.dot(xn_ref[...], w_ref[...], preferred_element_type=F32).astype(o_ref.dtype)


def norm_matmul(x, g, w, *, tm, tn, name):
    T, D = x.shape
    N = w.shape[1]
    return pl.pallas_call(
        _norm_matmul_kernel,
        grid=(T // tm, N // tn),
        in_specs=[pl.BlockSpec((tm, D), lambda i, j: (i, 0)),
                  pl.BlockSpec((1, D), lambda i, j: (0, 0)),
                  pl.BlockSpec((D, tn), lambda i, j: (0, j))],
        out_specs=pl.BlockSpec((tm, tn), lambda i, j: (i, j)),
        out_shape=jax.ShapeDtypeStruct((T, N), BF16),
        scratch_shapes=[pltpu.VMEM((tm, D), BF16)],
        compiler_params=_params("parallel", "arbitrary"),
        name=name,
    )(x, g.reshape(1, D), w)


def _rope_kernel(z_ref, c_ref, s_ref, p_ref, o_ref, *, n_q_groups):
    c = c_ref[...]
    s = s_ref[...]
    for gi in range(z_ref.shape[1] // LANES):
        cols = slice(gi * LANES, (gi + 1) * LANES)
        xg = z_ref[:, cols]
        swapped = jnp.dot(xg, p_ref[...], preferred_element_type=F32)
        r = xg.astype(F32) * c + swapped * s
        if gi < n_q_groups:
            r = r * (DA_HEAD_DIM ** -0.5)
        o_ref[:, cols] = r.astype(o_ref.dtype)


def rope_qk(z, cos_tab, sin_tab, perm, *, tm):
    T = z.shape[0]
    width = 2 * DA_HEADS * 2 * DA_HEAD_DIM
    return pl.pallas_call(
        functools.partial(_rope_kernel, n_q_groups=width // (2 * LANES)),
        grid=(T // tm,),
        in_specs=[pl.BlockSpec((tm, width), lambda i: (i, 0)),
                  pl.BlockSpec((tm, LANES), lambda i: (i, 0)),
                  pl.BlockSpec((tm, LANES), lambda i: (i, 0)),
                  pl.BlockSpec((LANES, LANES), lambda i: (0, 0))],
        out_specs=pl.BlockSpec((tm, width), lambda i: (i, 0)),
        out_shape=jax.ShapeDtypeStruct((T, width), BF16),
        compiler_params=_params("parallel"),
        name="rope_qk",
    )(z, cos_tab, sin_tab, perm)


def rope_tables(positions):
    half = ROT_DIM // 2
    inv = ROPE_THETA ** (-jnp.arange(0, ROT_DIM, 2, dtype=F32) / ROT_DIM)
    ang = positions.astype(F32).reshape(-1, 1) * inv
    cos, sin = jnp.cos(ang), jnp.sin(ang)
    T = ang.shape[0]
    ones = jnp.ones((T, DA_HEAD_DIM - ROT_DIM), F32)
    zeros = jnp.zeros((T, DA_HEAD_DIM - ROT_DIM), F32)
    c64 = jnp.concatenate([cos, cos, ones], axis=1)
    s64 = jnp.concatenate([-sin, sin, zeros], axis=1)
    lane = jnp.arange(LANES)
    d = lane % DA_HEAD_DIM
    src = jnp.where(d < half, lane + half, jnp.where(d < ROT_DIM, lane - half, -1))
    perm = (lane[:, None] == src[None, :]).astype(BF16)
    return jnp.tile(c64, (1, 2)), jnp.tile(s64, (1, 2)), perm


def _diff_attn_kernel(lam_ref, q_ref, k_ref, v_ref, g_ref, o_ref, *, tq, tk, lam_init):
    qi = pl.program_id(2)
    q = q_ref[...]
    lane = lax.broadcasted_iota(jnp.int32, q.shape, 1)
    zero = jnp.zeros_like(q)
    q_maps = (jnp.where(lane < DA_HEAD_DIM, q, zero), jnp.where(lane >= DA_HEAD_DIM, q, zero))
    row = lax.broadcasted_iota(jnp.int32, (tq, tk), 0)
    col = lax.broadcasted_iota(jnp.int32, (tq, tk), 1)
    per_q = tq // tk

    def block(kb, carry, diag_offset):
        start = pl.multiple_of(kb * tk, tk)
        k = k_ref[pl.ds(start, tk), :]
        v = v_ref[pl.ds(start, tk), :]
        new = []
        for mi in range(2):
            m, l, acc = carry[3 * mi:3 * mi + 3]
            s = lax.dot_general(q_maps[mi], k, (((1,), (1,)), ((), ())), preferred_element_type=F32)
            if diag_offset is not None:
                s = jnp.where(col + diag_offset <= row, s, -jnp.inf)
            m_new = jnp.maximum(m, jnp.max(s, axis=1, keepdims=True))
            p = jnp.exp(s - m_new)
            alpha = jnp.exp(m - m_new)
            l = alpha * l + jnp.sum(p, axis=1, keepdims=True)
            acc = alpha * acc + jnp.dot(p.astype(BF16), v, preferred_element_type=F32)
            new += [m_new, l, acc]
        return tuple(new)

    init = (jnp.full((tq, 1), -jnp.inf, F32), jnp.zeros((tq, 1), F32), jnp.zeros((tq, LANES), F32)) * 2
    carry = lax.fori_loop(0, qi * per_q, lambda kb, c: block(kb, c, None), init)
    for d in range(per_q):
        carry = block(qi * per_q + d, carry, d * tk)
    m1, l1, a1, m2, l2, a2 = carry
    o = a1 / l1 - lam_ref[0] * (a2 / l2)
    o_ref[...] = (_rmsnorm_rows(o, g_ref[...]) * (1.0 - lam_init)).astype(o_ref.dtype)


def diff_attention(qk, z, lam, subln_g, *, batch, seq, tq, tk, lam_init):
    T = qk.shape[0]
    nq = seq // tq
    width = DA_HEADS * 2 * DA_HEAD_DIM
    return pl.pallas_call(
        functools.partial(_diff_attn_kernel, tq=tq, tk=tk, lam_init=lam_init),
        grid=(batch, DA_HEADS, nq),
        in_specs=[pl.BlockSpec(memory_space=pltpu.SMEM),
                  pl.BlockSpec((tq, LANES), lambda b, h, i: (b * nq + i, h)),
                  pl.BlockSpec((seq, LANES), lambda b, h, i: (b, DA_HEADS + h)),
                  pl.BlockSpec((seq, LANES), lambda b, h, i: (b, 2 * DA_HEADS + h)),
                  pl.BlockSpec((1, LANES), lambda b, h, i: (0, 0))],
        out_specs=pl.BlockSpec((tq, LANES), lambda b, h, i: (b * nq + i, h)),
        out_shape=jax.ShapeDtypeStruct((T, width), BF16),
        compiler_params=_params("parallel", "parallel", "arbitrary"),
        name="diff_attention",
    )(lam, qk, qk, z, subln_g.reshape(1, LANES))


def _pool_kernel(p_ref, prev_ref, cur_band_ref, prev_band_ref, inv_ref, w_ref, scale_ref, o_ref):
    has_prev = pl.program_id(1) > 0
    for g in range(len(POOL_WINDOWS)):
        cols = slice(g * LANES, (g + 1) * LANES)
        pg = p_ref[:, cols]
        wsum = jnp.dot(cur_band_ref[g], pg, preferred_element_type=F32)
        halo = jnp.dot(prev_band_ref[g], prev_ref[:, cols], preferred_element_type=F32)
        wsum = wsum + jnp.where(has_prev, halo, 0.0)
        d = wsum * inv_ref[:, cols] - pg.astype(F32)
        y = jnp.dot(d.astype(BF16), w_ref[g], preferred_element_type=F32) * scale_ref[:, cols]
        o_ref[:, cols] = y.astype(o_ref.dtype)


def multiscale_pool(z, w_pool, pool_scale, *, batch, seq, tc):
    T = z.shape[0]
    nc = seq // tc
    width = len(POOL_WINDOWS) * LANES
    col_block = 3
    t = jnp.arange(tc)
    delta = t[:, None] - t[None, :]
    win = jnp.asarray(POOL_WINDOWS)[:, None, None]
    cur_band = ((delta >= 0) & (delta < win)).astype(BF16)
    prev_band = ((delta + tc >= 0) & (delta + tc < win)).astype(BF16)
    pos = jnp.arange(seq)
    inv_cnt = 1.0 / jnp.minimum(pos[:, None] + 1, jnp.repeat(jnp.asarray(POOL_WINDOWS), LANES)[None, :]).astype(F32)
    return pl.pallas_call(
        _pool_kernel,
        grid=(batch, nc),
        in_specs=[pl.BlockSpec((tc, width), lambda b, c: (b * nc + c, col_block)),
                  pl.BlockSpec((tc, width), lambda b, c: (b * nc + jnp.maximum(c - 1, 0), col_block)),
                  pl.BlockSpec((len(POOL_WINDOWS), tc, tc), lambda b, c: (0, 0, 0)),
                  pl.BlockSpec((len(POOL_WINDOWS), tc, tc), lambda b, c: (0, 0, 0)),
                  pl.BlockSpec((tc, width), lambda b, c: (c, 0)),
                  pl.BlockSpec((len(POOL_WINDOWS), LANES, LANES), lambda b, c: (0, 0, 0)),
                  pl.BlockSpec((1, width), lambda b, c: (0, 0))],
        out_specs=pl.BlockSpec((tc, width), lambda b, c: (b * nc + c, 0)),
        out_shape=jax.ShapeDtypeStruct((T, width), BF16),
        compiler_params=_params("parallel", "arbitrary"),
        name="multiscale_pool",
    )(z, z, cur_band, prev_band, inv_cnt, w_pool.astype(BF16), pool_scale.reshape(1, width))


def _sgate_kernel(u_ref, v_ref, g_ref, b_ref, ws_ref, bias_ref, o_ref):
    v = v_ref[...].astype(F32)
    mu = jnp.mean(v, axis=-1, keepdims=True)
    vc = v - mu
    var = jnp.mean(vc * vc, axis=-1, keepdims=True)
    vn = (vc * lax.rsqrt(var + EPS) * g_ref[...] + b_ref[...]).astype(BF16)
    for n in range(v.shape[0] // CHUNK):
        rows = slice(n * CHUNK, (n + 1) * CHUNK)
        for g in range(SG_GROUPS):
            cols = slice(g * LANES, (g + 1) * LANES)
            mixed = jnp.dot(ws_ref[g], vn[rows, cols], preferred_element_type=F32) + bias_ref[:, cols]
            o_ref[rows, cols] = (u_ref[rows, cols].astype(F32) * mixed).astype(o_ref.dtype)


def chunk_spatial_gate(z, ln_g, ln_b, w_s, b_s, *, tc):
    T = z.shape[0]
    width = SG_GROUPS * LANES
    ws = (w_s * jnp.tril(jnp.ones((CHUNK, CHUNK), w_s.dtype))).astype(BF16)
    bias = jnp.repeat(b_s.T, LANES, axis=1)
    return pl.pallas_call(
        _sgate_kernel,
        grid=(T // tc,),
        in_specs=[pl.BlockSpec((tc, width), lambda i: (i, 4)),
                  pl.BlockSpec((tc, width), lambda i: (i, 5)),
                  pl.BlockSpec((1, width), lambda i: (0, 0)),
                  pl.BlockSpec((1, width), lambda i: (0, 0)),
                  pl.BlockSpec((SG_GROUPS, CHUNK, CHUNK), lambda i: (0, 0, 0)),
                  pl.BlockSpec((CHUNK, width), lambda i: (0, 0))],
        out_specs=pl.BlockSpec((tc, width), lambda i: (i, 0)),
        out_shape=jax.ShapeDtypeStruct((T, width), BF16),
        compiler_params=_params("parallel"),
        name="chunk_spatial_gate",
    )(z, z, ln_g.reshape(1, width), ln_b.reshape(1, width), ws, bias)


def _merge_kernel(a_ref, p_ref, s_ref, gate_ref, x_ref, wb_ref, wo_ref, o_ref):
    d = x_ref.shape[1]
    merged = None
    for n, br in enumerate((a_ref, p_ref, s_ref)):
        proj = jnp.dot(br[...], wb_ref[n], preferred_element_type=F32)
        gate = jax.nn.sigmoid(gate_ref[:, n * d:(n + 1) * d].astype(F32))
        merged = gate * proj if merged is None else merged + gate * proj
    o_ref[...] = x_ref[...] + jnp.dot(merged.astype(BF16), wo_ref[...], preferred_element_type=F32)


def branch_merge(a_out, p_out, s_out, z, x, w_branch, w_out, *, tm):
    T, D = x.shape
    bw = a_out.shape[1]
    br_spec = pl.BlockSpec((tm, bw), lambda i: (i, 0))
    return pl.pallas_call(
        _merge_kernel,
        grid=(T // tm,),
        in_specs=[br_spec, br_spec, br_spec,
                  pl.BlockSpec((tm, N_BRANCH * D), lambda i: (i, 1)),
                  pl.BlockSpec((tm, D), lambda i: (i, 0)),
                  pl.BlockSpec((N_BRANCH, bw, D), lambda i: (0, 0, 0)),
                  pl.BlockSpec((D, D), lambda i: (0, 0))],
        out_specs=pl.BlockSpec((tm, D), lambda i: (i, 0)),
        out_shape=jax.ShapeDtypeStruct((T, D), F32),
        compiler_params=_params("parallel"),
        name="branch_merge",
    )(a_out, p_out, s_out, z, x, w_branch.astype(BF16), w_out.astype(BF16))


def _cross_kernel(x_ref, g_ref, k_ref, v_ref, wq_ref, wo_ref, o_ref):
    x = x_ref[...]
    d = x.shape[1]
    hd = d // X_HEADS
    xn = _rmsnorm_rows(x, g_ref[...]).astype(BF16)
    q = jnp.dot(xn, wq_ref[...], preferred_element_type=F32).astype(BF16)
    heads = []
    for h in range(X_HEADS):
        cols = slice(h * hd, (h + 1) * hd)
        s = lax.dot_general(q[:, cols], k_ref[:, cols], (((1,), (1,)), ((), ())), preferred_element_type=F32)
        s = s * (hd ** -0.5)
        p = jnp.exp(s - jnp.max(s, axis=1, keepdims=True))
        p = p / jnp.sum(p, axis=1, keepdims=True)
        heads.append(jnp.dot(p.astype(BF16), v_ref[:, cols], preferred_element_type=F32).astype(BF16))
    o = jnp.concatenate(heads, axis=1)
    o_ref[...] = x + jnp.dot(o, wo_ref[...], preferred_element_type=F32)


def cross_attention(x, g, kv, w_q, w_o, *, batch, seq, tm):
    T, D = x.shape
    M = kv.shape[0] // batch
    nb = seq // tm
    return pl.pallas_call(
        _cross_kernel,
        grid=(batch, nb),
        in_specs=[pl.BlockSpec((tm, D), lambda b, i: (b * nb + i, 0)),
                  pl.BlockSpec((1, D), lambda b, i: (0, 0)),
                  pl.BlockSpec((M, D), lambda b, i: (b, 0)),
                  pl.BlockSpec((M, D), lambda b, i: (b, 1)),
                  pl.BlockSpec((D, D), lambda b, i: (0, 0)),
                  pl.BlockSpec((D, D), lambda b, i: (0, 0))],
        out_specs=pl.BlockSpec((tm, D), lambda b, i: (b * nb + i, 0)),
        out_shape=jax.ShapeDtypeStruct((T, D), F32),
        compiler_params=_params("parallel", "parallel"),
        name="cross_attention",
    )(x, g.reshape(1, D), kv, kv, w_q.astype(BF16), w_o.astype(BF16))


def _peer_scores_kernel(x_ref, g_ref, wq_ref, keys_ref, o_ref):
    xn = _rmsnorm_rows(x_ref[...], g_ref[...]).astype(BF16)
    q = jnp.dot(xn, wq_ref[...], preferred_element_type=F32).astype(BF16)
    _, nk, nh, tm = o_ref.shape
    width = keys_ref.shape[2]
    for side in range(2):
        st = lax.dot_general(keys_ref[side], q[:, side * width:(side + 1) * width],
                             (((1,), (1,)), ((), ())), preferred_element_type=F32)
        o_ref[side] = st.reshape(nk, nh, tm)


def peer_scores(x, g, w_pq, sub_keys, *, tm):
    T, D = x.shape
    nh, _, nk, half = sub_keys.shape
    wq = w_pq.reshape(D, nh, 2, half).transpose(0, 2, 1, 3).reshape(D, 2 * nh * half).astype(BF16)
    keys = jnp.einsum('hsnk,hg->snhgk', sub_keys, jnp.eye(nh, dtype=sub_keys.dtype))
    keys = keys.reshape(2, nk * nh, nh * half).astype(BF16)
    return pl.pallas_call(
        _peer_scores_kernel,
        grid=(T // tm,),
        in_specs=[pl.BlockSpec((tm, D), lambda i: (i, 0)),
                  pl.BlockSpec((1, D), lambda i: (0, 0)),
                  pl.BlockSpec((D, 2 * nh * half), lambda i: (0, 0)),
                  pl.BlockSpec((2, nk * nh, nh * half), lambda i: (0, 0, 0))],
        out_specs=pl.BlockSpec((2, nk, nh, tm), lambda i: (0, 0, 0, i)),
        out_shape=jax.ShapeDtypeStruct((2, nk, nh, T), F32),
        compiler_params=_params("parallel"),
        name="peer_scores",
    )(x, g.reshape(1, D), wq, keys)


def _first_argmax(xs):
    nodes = [(xs[i], float(i)) for i in range(len(xs))]
    while len(nodes) > 1:
        merged = []
        for j in range(0, len(nodes) - 1, 2):
            (va, ia), (vb, ib) = nodes[j], nodes[j + 1]
            merged.append((jnp.maximum(va, vb), jnp.where(va >= vb, ia, ib)))
        if len(nodes) % 2:
            merged.append(nodes[-1])
        nodes = merged
    return nodes[0]


def _peer_route_kernel(s_ref, rank_ref, e2_ref, cnt_ref, c1_ref, work_ref, top_ref, arg_ref):
    nk = s_ref.shape[1]
    shape = s_ref.shape[1:]
    key_iota = lax.broadcasted_iota(jnp.int32, shape, 0).astype(F32)

    for side in range(2):
        work_ref[...] = s_ref[side]

        def extract(r, _, side=side):
            s = work_ref[...]
            m, idx = _first_argmax(s)
            work_ref[...] = jnp.where(key_iota == idx[None], -jnp.inf, s)
            top_ref[side, r] = m
            arg_ref[side, r] = idx
            return 0

        lax.fori_loop(0, PEER_TOPK, extract, 0)

    t0 = [top_ref[0, a] for a in range(PEER_TOPK)]
    t1 = [top_ref[1, b] for b in range(PEER_TOPK)]
    cand = [t0[a] + t1[b] for a, b in PEER_CELLS]
    best = cand[0]
    live = list(cand)
    chosen = [None] * len(PEER_CELLS)
    for _ in range(PEER_TOPK):
        _, first = _first_argmax(live)
        for ci in range(len(PEER_CELLS)):
            hit = first == float(ci)
            chosen[ci] = hit if chosen[ci] is None else chosen[ci] | hit
            live[ci] = jnp.where(hit, -jnp.inf, live[ci])
    denom = None
    cnt_a = [jnp.zeros(best.shape, F32) for _ in range(PEER_TOPK)]
    for ci, (a, b) in enumerate(PEER_CELLS):
        term = jnp.where(chosen[ci], jnp.exp(cand[ci] - best), 0.0)
        denom = term if denom is None else denom + term
        cnt_a[a] = cnt_a[a] + jnp.where(chosen[ci], 1.0, 0.0)
    inv_denom = 1.0 / denom

    cnt = jnp.zeros(shape, F32)
    rank = jnp.full(shape, float(PEER_TOPK), F32)
    for r in range(PEER_TOPK):
        cnt = jnp.where(key_iota == arg_ref[0, r][None], cnt_a[r][None], cnt)
        rank = jnp.where(key_iota == arg_ref[1, r][None], float(r), rank)
    cnt_ref[:, 0] = cnt
    c1_ref[:, 0] = jnp.exp(s_ref[0] - t0[0][None]) * (0.5 * inv_denom)[None]
    rank_ref[...] = rank
    e2_ref[...] = jnp.exp(s_ref[1] - t1[0][None])


def peer_route(scores):
    _, nk, nh, T = scores.shape
    tb = LANES
    spec = pl.BlockSpec((nk, nh, tb), lambda i: (0, 0, i))
    out = jax.ShapeDtypeStruct((nk, nh, T), F32)
    row_spec = pl.BlockSpec((nk, 1, nh, tb), lambda i: (0, i, 0, 0))
    row_out = jax.ShapeDtypeStruct((nk, T // tb, nh, tb), F32)
    return pl.pallas_call(
        _peer_route_kernel,
        grid=(T // tb,),
        in_specs=[pl.BlockSpec((2, nk, nh, tb), lambda i: (0, 0, 0, i))],
        out_specs=[spec, spec, row_spec, row_spec],
        out_shape=[out, out, row_out, row_out],
        scratch_shapes=[pltpu.VMEM((nk, nh, tb), F32),
                        pltpu.VMEM((2, PEER_TOPK, nh, tb), F32),
                        pltpu.VMEM((2, PEER_TOPK, nh, tb), F32)],
        compiler_params=_params("parallel"),
        name="peer_route",
    )(scores)


def _row_tile(ref, tile, hd, rows):
    parts = [ref[tile, q, pl.ds(hd, rows, stride=0), :] for q in range(ref.shape[1])]
    return jnp.concatenate(parts, axis=-1).astype(BF16)


def _peer_dense_kernel(x_ref, g_ref, u_ref, vt_ref, rank_ref, e2_ref, cnt_ref, c1_ref, o_ref, xnt_ref, acc_ref, *,
                       chunk):
    e = pl.program_id(1)

    @pl.when(e == 0)
    def _():
        xn = _rmsnorm_rows(x_ref[...], g_ref[...])
        xnt_ref[...] = xn.T.astype(BF16)
        acc_ref[...] = jnp.zeros_like(acc_ref)

    nh, groups, rows, tb = rank_ref.shape
    nk = groups * rows
    zero = jnp.zeros((groups, rows, tb), BF16)
    xnt = xnt_ref[...]
    total = None
    for c in range(u_ref.shape[0] // chunk):
        h = jnp.dot(u_ref[c * chunk:(c + 1) * chunk, :], xnt, preferred_element_type=F32)
        act = (h * (1.0 + lax.erf(h * (2.0 ** -0.5)))).astype(BF16)
        n_tiles = chunk // nk
        act = act.reshape(n_tiles, groups, rows, tb)
        w = [None] * n_tiles
        for hd in range(nh):
            rank = rank_ref[hd]
            e2 = e2_ref[hd]
            for ii in range(n_tiles):
                tile = c * n_tiles + ii
                cnt = _row_tile(cnt_ref, tile, hd, rows)
                c1 = _row_tile(c1_ref, tile, hd, rows)
                term = jnp.where(rank < cnt[None], e2 * c1[None], zero)
                w[ii] = term if w[ii] is None else w[ii] + term
        a = jnp.stack([act[ii] * w[ii] for ii in range(n_tiles)], axis=0).reshape(chunk, tb)
        part = jnp.dot(vt_ref[:, c * chunk:(c + 1) * chunk], a, preferred_element_type=F32)
        total = part if total is None else total + part
    acc_ref[...] += total

    @pl.when(e == pl.num_programs(1) - 1)
    def _():
        o_ref[...] = x_ref[...] + acc_ref[...].T


def peer_dense(x, g, u, vt, rank, e2, cnt, c1, *, tb, te, chunk):
    T, D = x.shape
    E = u.shape[0]
    nh, nk, _ = rank.shape
    rank = rank.reshape(nh, nk // BF16_ROWS, BF16_ROWS, T)
    e2 = e2.reshape(nh, nk // BF16_ROWS, BF16_ROWS, T)
    tab_spec = pl.BlockSpec((nh, nk // BF16_ROWS, BF16_ROWS, tb), lambda i, e: (0, 0, 0, i))
    row_spec = pl.BlockSpec((te // nk, tb // LANES, nh, LANES), lambda i, e: (e, i, 0, 0))
    return pl.pallas_call(
        functools.partial(_peer_dense_kernel, chunk=chunk),
        grid=(T // tb, E // te),
        in_specs=[pl.BlockSpec((tb, D), lambda i, e: (i, 0)),
                  pl.BlockSpec((1, D), lambda i, e: (0, 0)),
                  pl.BlockSpec((te, D), lambda i, e: (e, 0)),
                  pl.BlockSpec((D, te), lambda i, e: (0, e)),
                  tab_spec, tab_spec, row_spec, row_spec],
        out_specs=pl.BlockSpec((tb, D), lambda i, e: (i, 0)),
        out_shape=jax.ShapeDtypeStruct((T, D), F32),
        scratch_shapes=[pltpu.VMEM((D, tb), BF16), pltpu.VMEM((D, tb), F32)],
        compiler_params=_params("parallel", "arbitrary"),
        name="peer_dense",
    )(x, g.reshape(1, D), u, vt, rank, e2, cnt, c1)


def _expert_tables_kernel(u_ref, v_ref, ub_ref, vt_ref):
    ub_ref[...] = u_ref[0].astype(ub_ref.dtype)
    vt_ref[...] = v_ref[0].T.astype(vt_ref.dtype)


def expert_tables(peer_u, peer_v, layer, *, te):
    _, E, D = peer_u.shape
    in_spec = pl.BlockSpec((1, te, D), lambda e: (layer, e, 0))
    return pl.pallas_call(
        _expert_tables_kernel,
        grid=(E // te,),
        in_specs=[in_spec, in_spec],
        out_specs=[pl.BlockSpec((te, D), lambda e: (e, 0)), pl.BlockSpec((D, te), lambda e: (0, e))],
        out_shape=[jax.ShapeDtypeStruct((E, D), BF16), jax.ShapeDtypeStruct((D, E), BF16)],
        compiler_params=_params("parallel"),
        name="expert_tables",
    )(peer_u, peer_v)


def _final_norm_kernel(x_ref, g_ref, o_ref):
    o_ref[...] = _rmsnorm_rows(x_ref[...], g_ref[...])


def final_norm(x, g, *, tm):
    T, D = x.shape
    return pl.pallas_call(
        _final_norm_kernel,
        grid=(T // tm,),
        in_specs=[pl.BlockSpec((tm, D), lambda i: (i, 0)), pl.BlockSpec((1, D), lambda i: (0, 0))],
        out_specs=pl.BlockSpec((tm, D), lambda i: (i, 0)),
        out_shape=jax.ShapeDtypeStruct((T, D), F32),
        compiler_params=_params("parallel"),
        name="final_norm",
    )(x, g.reshape(1, D))


def _tile(n, want):
    if n <= want:
        return n
    t = want - want % LANES
    while n % t:
        t -= LANES
    return t


def kernel(x, mem, positions, g_mix, w_in, lam_qk, subln_g, w_pool, pool_scale, sg_ln_g, sg_ln_b, w_spatial,
           b_spatial, w_branch, w_out, g_cross, g_mem, w_xq, w_xkv, w_xo, g_ffn, w_pq, sub_keys, peer_u, peer_v,
           g_final):
    B, S, D = x.shape
    T = B * S
    depth = w_in.shape[0]
    x = x.reshape(T, D)
    memf = mem.reshape(-1, D)
    cos_tab, sin_tab, perm = rope_tables(positions)

    tm = _tile(S, 1024)
    tb_dense = _tile(S, 512)

    for l in range(depth):
        lam_init = 0.8 - 0.6 * math.exp(-0.3 * l)
        lq = lam_qk[l].astype(F32)
        lam = (jnp.exp(jnp.sum(lq[0] * lq[1])) - jnp.exp(jnp.sum(lq[2] * lq[3])) + lam_init).reshape(1)

        z = norm_matmul(x, g_mix[l], w_in[l].astype(BF16), tm=tm, tn=1536, name="in_proj")
        qk = rope_qk(z, cos_tab, sin_tab, perm, tm=tm)
        a_out = diff_attention(qk, z, lam, subln_g[l], batch=B, seq=S, tq=_tile(S, 1024), tk=_tile(S, 512),
                               lam_init=lam_init)
        p_out = multiscale_pool(z, w_pool[l], pool_scale[l], batch=B, seq=S, tc=_tile(S, 256))
        s_out = chunk_spatial_gate(z, sg_ln_g[l], sg_ln_b[l], w_spatial[l], b_spatial[l], tc=_tile(S, 512))
        x = branch_merge(a_out, p_out, s_out, z, x, w_branch[l], w_out[l], tm=_tile(S, 512))

        kv = norm_matmul(memf, g_mem[l], w_xkv[l].astype(BF16), tm=_tile(memf.shape[0], 1024), tn=1024,
                         name="mem_kv")
        x = cross_attention(x, g_cross[l], kv, w_xq[l], w_xo[l], batch=B, seq=S, tm=_tile(S, 512))

        scores = peer_scores(x, g_ffn[l], w_pq[l], sub_keys[l], tm=_tile(S, 512))
        rank, e2, cnt, c1 = peer_route(scores)
        u_bf, vt_bf = expert_tables(peer_u, peer_v, l, te=512)
        x = peer_dense(x, g_ffn[l], u_bf, vt_bf,
                       rank.transpose(1, 0, 2).astype(BF16), e2.transpose(1, 0, 2).astype(BF16), cnt, c1,
                       tb=tb_dense, te=2048, chunk=2048)

    return final_norm(x, g_final, tm=tm).reshape(B, S, D)
```

```python
import functools
import math

import jax
import jax.numpy as jnp
from jax import lax
from jax.experimental import pallas as pl
from jax.experimental.pallas import tpu as pltpu

F32 = jnp.float32
BF16 = jnp.bfloat16

EPS = 1e-6
LANES = 128
BF16_ROWS = 16
VMEM_LIMIT_BYTES = 48 * 1024 * 1024

DA_HEADS = 4
DA_HEAD_DIM = 64
ROT_DIM = DA_HEAD_DIM // 4
ROPE_THETA = 500000.0
POOL_WINDOWS = (2, 4, 8, 16)
CHUNK = 128
SG_GROUPS = 4
N_BRANCH = 3
X_HEADS = 4
PEER_TOPK = 16
PEER_CELLS = tuple((a, b) for a in range(PEER_TOPK) for b in range(PEER_TOPK) if (a + 1) * (b + 1) <= PEER_TOPK)


def _params(*sem):
    return pltpu.CompilerParams(dimension_semantics=sem, vmem_limit_bytes=VMEM_LIMIT_BYTES)


def _rmsnorm_rows(x, g):
    ms = jnp.mean(x * x, axis=-1, keepdims=True)
    return x * lax.rsqrt(ms + EPS) * g


def _norm_matmul_kernel(x_ref, g_ref, w_ref, o_ref, xn_ref):
    @pl.when(pl.program_id(1) == 0)
    def _():
        xn_ref[...] = _rmsnorm_rows(x_ref[...], g_ref[...]).astype(BF16)

    o_ref[...] = jnp.dot(xn_ref[...], w_ref[...], preferred_element_type=F32).astype(o_ref.dtype)


def norm_matmul(x, g, w, *, tm, tn, name):
    T, D = x.shape
    N = w.shape[1]
    return pl.pallas_call(
        _norm_matmul_kernel,
        grid=(T // tm, N // tn),
        in_specs=[pl.BlockSpec((tm, D), lambda i, j: (i, 0)),
                  pl.BlockSpec((1, D), lambda i, j: (0, 0)),
                  pl.BlockSpec((D, tn), lambda i, j: (0, j))],
        out_specs=pl.BlockSpec((tm, tn), lambda i, j: (i, j)),
        out_shape=jax.ShapeDtypeStruct((T, N), BF16),
        scratch_shapes=[pltpu.VMEM((tm, D), BF16)],
        compiler_params=_params("parallel", "arbitrary"),
        name=name,
    )(x, g.reshape(1, D), w)


def _rope_kernel(z_ref, c_ref, s_ref, p_ref, o_ref, *, n_q_groups):
    c = c_ref[...]
    s = s_ref[...]
    for gi in range(z_ref.shape[1] // LANES):
        cols = slice(gi * LANES, (gi + 1) * LANES)
        xg = z_ref[:, cols]
        swapped = jnp.dot(xg, p_ref[...], preferred_element_type=F32)
        r = xg.astype(F32) * c + swapped * s
        if gi < n_q_groups:
            r = r * (DA_HEAD_DIM ** -0.5)
        o_ref[:, cols] = r.astype(o_ref.dtype)


def rope_qk(z, cos_tab, sin_tab, perm, *, tm):
    T = z.shape[0]
    width = 2 * DA_HEADS * 2 * DA_HEAD_DIM
    return pl.pallas_call(
        functools.partial(_rope_kernel, n_q_groups=width // (2 * LANES)),
        grid=(T // tm,),
        in_specs=[pl.BlockSpec((tm, width), lambda i: (i, 0)),
                  pl.BlockSpec((tm, LANES), lambda i: (i, 0)),
                  pl.BlockSpec((tm, LANES), lambda i: (i, 0)),
                  pl.BlockSpec((LANES, LANES), lambda i: (0, 0))],
        out_specs=pl.BlockSpec((tm, width), lambda i: (i, 0)),
        out_shape=jax.ShapeDtypeStruct((T, width), BF16),
        compiler_params=_params("parallel"),
        name="rope_qk",
    )(z, cos_tab, sin_tab, perm)


def rope_tables(positions):
    half = ROT_DIM // 2
    inv = ROPE_THETA ** (-jnp.arange(0, ROT_DIM, 2, dtype=F32) / ROT_DIM)
    ang = positions.astype(F32).reshape(-1, 1) * inv
    cos, sin = jnp.cos(ang), jnp.sin(ang)
    T = ang.shape[0]
    ones = jnp.ones((T, DA_HEAD_DIM - ROT_DIM), F32)
    zeros = jnp.zeros((T, DA_HEAD_DIM - ROT_DIM), F32)
    c64 = jnp.concatenate([cos, cos, ones], axis=1)
    s64 = jnp.concatenate([-sin, sin, zeros], axis=1)
    lane = jnp.arange(LANES)
    d = lane % DA_HEAD_DIM
    src = jnp.where(d < half, lane + half, jnp.where(d < ROT_DIM, lane - half, -1))
    perm = (lane[:, None] == src[None, :]).astype(BF16)
    return jnp.tile(c64, (1, 2)), jnp.tile(s64, (1, 2)), perm


def _diff_attn_kernel(lam_ref, q_ref, k_ref, v_ref, g_ref, o_ref, *, tq, tk, lam_init):
    qi = pl.program_id(2)
    q = q_ref[...]
    lane = lax.broadcasted_iota(jnp.int32, q.shape, 1)
    zero = jnp.zeros_like(q)
    q_maps = (jnp.where(lane < DA_HEAD_DIM, q, zero), jnp.where(lane >= DA_HEAD_DIM, q, zero))
    row = lax.broadcasted_iota(jnp.int32, (tq, tk), 0)
    col = lax.broadcasted_iota(jnp.int32, (tq, tk), 1)
    per_q = tq // tk

    def block(kb, carry, diag_offset):
        start = pl.multiple_of(kb * tk, tk)
        k = k_ref[pl.ds(start, tk), :]
        v = v_ref[pl.ds(start, tk), :]
        new = []
        for mi in range(2):
            m, l, acc = carry[3 * mi:3 * mi + 3]
            s = lax.dot_general(q_maps[mi], k, (((1,), (1,)), ((), ())), preferred_element_type=F32)
            if diag_offset is not None:
                s = jnp.where(col + diag_offset <= row, s, -jnp.inf)
            m_new = jnp.maximum(m, jnp.max(s, axis=1, keepdims=True))
            p = jnp.exp(s - m_new)
            alpha = jnp.exp(m - m_new)
            l = alpha * l + jnp.sum(p, axis=1, keepdims=True)
            acc = alpha * acc + jnp.dot(p.astype(BF16), v, preferred_element_type=F32)
            new += [m_new, l, acc]
        return tuple(new)

    init = (jnp.full((tq, 1), -jnp.inf, F32), jnp.zeros((tq, 1), F32), jnp.zeros((tq, LANES), F32)) * 2
    carry = lax.fori_loop(0, qi * per_q, lambda kb, c: block(kb, c, None), init)
    for d in range(per_q):
        carry = block(qi * per_q + d, carry, d * tk)
    m1, l1, a1, m2, l2, a2 = carry
    o = a1 / l1 - lam_ref[0] * (a2 / l2)
    o_ref[...] = (_rmsnorm_rows(o, g_ref[...]) * (1.0 - lam_init)).astype(o_ref.dtype)


def diff_attention(qk, z, lam, subln_g, *, batch, seq, tq, tk, lam_init):
    T = qk.shape[0]
    nq = seq // tq
    width = DA_HEADS * 2 * DA_HEAD_DIM
    return pl.pallas_call(
        functools.partial(_diff_attn_kernel, tq=tq, tk=tk, lam_init=lam_init),
        grid=(batch, DA_HEADS, nq),
        in_specs=[pl.BlockSpec(memory_space=pltpu.SMEM),
                  pl.BlockSpec((tq, LANES), lambda b, h, i: (b * nq + i, h)),
                  pl.BlockSpec((seq, LANES), lambda b, h, i: (b, DA_HEADS + h)),
                  pl.BlockSpec((seq, LANES), lambda b, h, i: (b, 2 * DA_HEADS + h)),
                  pl.BlockSpec((1, LANES), lambda b, h, i: (0, 0))],
        out_specs=pl.BlockSpec((tq, LANES), lambda b, h, i: (b * nq + i, h)),
        out_shape=jax.ShapeDtypeStruct((T, width), BF16),
        compiler_params=_params("parallel", "parallel", "arbitrary"),
        name="diff_attention",
    )(lam, qk, qk, z, subln_g.reshape(1, LANES))


def _pool_kernel(p_ref, prev_ref, cur_band_ref, prev_band_ref, inv_ref, w_ref, scale_ref, o_ref):
    has_prev = pl.program_id(1) > 0
    for g in range(len(POOL_WINDOWS)):
        cols = slice(g * LANES, (g + 1) * LANES)
        pg = p_ref[:, cols]
        wsum = jnp.dot(cur_band_ref[g], pg, preferred_element_type=F32)
        halo = jnp.dot(prev_band_ref[g], prev_ref[:, cols], preferred_element_type=F32)
        wsum = wsum + jnp.where(has_prev, halo, 0.0)
        d = wsum * inv_ref[:, cols] - pg.astype(F32)
        y = jnp.dot(d.astype(BF16), w_ref[g], preferred_element_type=F32) * scale_ref[:, cols]
        o_ref[:, cols] = y.astype(o_ref.dtype)


def multiscale_pool(z, w_pool, pool_scale, *, batch, seq, tc):
    T = z.shape[0]
    nc = seq // tc
    width = len(POOL_WINDOWS) * LANES
    col_block = 3
    t = jnp.arange(tc)
    delta = t[:, None] - t[None, :]
    win = jnp.asarray(POOL_WINDOWS)[:, None, None]
    cur_band = ((delta >= 0) & (delta < win)).astype(BF16)
    prev_band = ((delta + tc >= 0) & (delta + tc < win)).astype(BF16)
    pos = jnp.arange(seq)
    inv_cnt = 1.0 / jnp.minimum(pos[:, None] + 1, jnp.repeat(jnp.asarray(POOL_WINDOWS), LANES)[None, :]).astype(F32)
    return pl.pallas_call(
        _pool_kernel,
        grid=(batch, nc),
        in_specs=[pl.BlockSpec((tc, width), lambda b, c: (b * nc + c, col_block)),
                  pl.BlockSpec((tc, width), lambda b, c: (b * nc + jnp.maximum(c - 1, 0), col_block)),
                  pl.BlockSpec((len(POOL_WINDOWS), tc, tc), lambda b, c: (0, 0, 0)),
                  pl.BlockSpec((len(POOL_WINDOWS), tc, tc), lambda b, c: (0, 0, 0)),
                  pl.BlockSpec((tc, width), lambda b, c: (c, 0)),
                  pl.BlockSpec((len(POOL_WINDOWS), LANES, LANES), lambda b, c: (0, 0, 0)),
                  pl.BlockSpec((1, width), lambda b, c: (0, 0))],
        out_specs=pl.BlockSpec((tc, width), lambda b, c: (b * nc + c, 0)),
        out_shape=jax.ShapeDtypeStruct((T, width), BF16),
        compiler_params=_params("parallel", "arbitrary"),
        name="multiscale_pool",
    )(z, z, cur_band, prev_band, inv_cnt, w_pool.astype(BF16), pool_scale.reshape(1, width))


def _sgate_kernel(u_ref, v_ref, g_ref, b_ref, ws_ref, bias_ref, o_ref):
    v = v_ref[...].astype(F32)
    mu = jnp.mean(v, axis=-1, keepdims=True)
    vc = v - mu
    var = jnp.mean(vc * vc, axis=-1, keepdims=True)
    vn = (vc * lax.rsqrt(var + EPS) * g_ref[...] + b_ref[...]).astype(BF16)
    for n in range(v.shape[0] // CHUNK):
        rows = slice(n * CHUNK, (n + 1) * CHUNK)
        for g in range(SG_GROUPS):
            cols = slice(g * LANES, (g + 1) * LANES)
            mixed = jnp.dot(ws_ref[g], vn[rows, cols], preferred_element_type=F32) + bias_ref[:, cols]
            o_ref[rows, cols] = (u_ref[rows, cols].astype(F32) * mixed).astype(o_ref.dtype)


def chunk_spatial_gate(z, ln_g, ln_b, w_s, b_s, *, tc):
    T = z.shape[0]
    width = SG_GROUPS * LANES
    ws = (w_s * jnp.tril(jnp.ones((CHUNK, CHUNK), w_s.dtype))).astype(BF16)
    bias = jnp.repeat(b_s.T, LANES, axis=1)
    return pl.pallas_call(
        _sgate_kernel,
        grid=(T // tc,),
        in_specs=[pl.BlockSpec((tc, width), lambda i: (i, 4)),
                  pl.BlockSpec((tc, width), lambda i: (i, 5)),
                  pl.BlockSpec((1, width), lambda i: (0, 0)),
                  pl.BlockSpec((1, width), lambda i: (0, 0)),
                  pl.BlockSpec((SG_GROUPS, CHUNK, CHUNK), lambda i: (0, 0, 0)),
                  pl.BlockSpec((CHUNK, width), lambda i: (0, 0))],
        out_specs=pl.BlockSpec((tc, width), lambda i: (i, 0)),
        out_shape=jax.ShapeDtypeStruct((T, width), BF16),
        compiler_params=_params("parallel"),
        name="chunk_spatial_gate",
    )(z, z, ln_g.reshape(1, width), ln_b.reshape(1, width), ws, bias)


def _merge_kernel(a_ref, p_ref, s_ref, gate_ref, x_ref, wb_ref, wo_ref, o_ref):
    d = x_ref.shape[1]
    merged = None
    for n, br in enumerate((a_ref, p_ref, s_ref)):
        proj = jnp.dot(br[...], wb_ref[n], preferred_element_type=F32)
        gate = jax.nn.sigmoid(gate_ref[:, n * d:(n + 1) * d].astype(F32))
        merged = gate * proj if merged is None else merged + gate * proj
    o_ref[...] = x_ref[...] + jnp.dot(merged.astype(BF16), wo_ref[...], preferred_element_type=F32)


def branch_merge(a_out, p_out, s_out, z, x, w_branch, w_out, *, tm):
    T, D = x.shape
    bw = a_out.shape[1]
    br_spec = pl.BlockSpec((tm, bw), lambda i: (i, 0))
    return pl.pallas_call(
        _merge_kernel,
        grid=(T // tm,),
        in_specs=[br_spec, br_spec, br_spec,
                  pl.BlockSpec((tm, N_BRANCH * D), lambda i: (i, 1)),
                  pl.BlockSpec((tm, D), lambda i: (i, 0)),
                  pl.BlockSpec((N_BRANCH, bw, D), lambda i: (0, 0, 0)),
                  pl.BlockSpec((D, D), lambda i: (0, 0))],
        out_specs=pl.BlockSpec((tm, D), lambda i: (i, 0)),
        out_shape=jax.ShapeDtypeStruct((T, D), F32),
        compiler_params=_params("parallel"),
        name="branch_merge",
    )(a_out, p_out, s_out, z, x, w_branch.astype(BF16), w_out.astype(BF16))


def _cross_kernel(x_ref, g_ref, k_ref, v_ref, wq_ref, wo_ref, o_ref):
    x = x_ref[...]
    d = x.shape[1]
    hd = d // X_HEADS
    xn = _rmsnorm_rows(x, g_ref[...]).astype(BF16)
    q = jnp.dot(xn, wq_ref[...], preferred_element_type=F32).astype(BF16)
    heads = []
    for h in range(X_HEADS):
        cols = slice(h * hd, (h + 1) * hd)
        s = lax.dot_general(q[:, cols], k_ref[:, cols], (((1,), (1,)), ((), ())), preferred_element_type=F32)
        s = s * (hd ** -0.5)
        p = jnp.exp(s - jnp.max(s, axis=1, keepdims=True))
        p = p / jnp.sum(p, axis=1, keepdims=True)
        heads.append(jnp.dot(p.astype(BF16), v_ref[:, cols], preferred_element_type=F32).astype(BF16))
    o = jnp.concatenate(heads, axis=1)
    o_ref[...] = x + jnp.dot(o, wo_ref[...], preferred_element_type=F32)


def cross_attention(x, g, kv, w_q, w_o, *, batch, seq, tm):
    T, D = x.shape
    M = kv.shape[0] // batch
    nb = seq // tm
    return pl.pallas_call(
        _cross_kernel,
        grid=(batch, nb),
        in_specs=[pl.BlockSpec((tm, D), lambda b, i: (b * nb + i, 0)),
                  pl.BlockSpec((1, D), lambda b, i: (0, 0)),
                  pl.BlockSpec((M, D), lambda b, i: (b, 0)),
                  pl.BlockSpec((M, D), lambda b, i: (b, 1)),
                  pl.BlockSpec((D, D), lambda b, i: (0, 0)),
                  pl.BlockSpec((D, D), lambda b, i: (0, 0))],
        out_specs=pl.BlockSpec((tm, D), lambda b, i: (b * nb + i, 0)),
        out_shape=jax.ShapeDtypeStruct((T, D), F32),
        compiler_params=_params("parallel", "parallel"),
        name="cross_attention",
    )(x, g.reshape(1, D), kv, kv, w_q.astype(BF16), w_o.astype(BF16))


def _peer_scores_kernel(x_ref, g_ref, wq_ref, keys_ref, o_ref):
    xn = _rmsnorm_rows(x_ref[...], g_ref[...]).astype(BF16)
    q = jnp.dot(xn, wq_ref[...], preferred_element_type=F32).astype(BF16)
    _, nk, nh, tm = o_ref.shape
    width = keys_ref.shape[2]
    for side in range(2):
        st = lax.dot_general(keys_ref[side], q[:, side * width:(side + 1) * width],
                             (((1,), (1,)), ((), ())), preferred_element_type=F32)
        o_ref[side] = st.reshape(nk, nh, tm)


def peer_scores(x, g, w_pq, sub_keys, *, tm):
    T, D = x.shape
    nh, _, nk, half = sub_keys.shape
    wq = w_pq.reshape(D, nh, 2, half).transpose(0, 2, 1, 3).reshape(D, 2 * nh * half).astype(BF16)
    keys = jnp.einsum('hsnk,hg->snhgk', sub_keys, jnp.eye(nh, dtype=sub_keys.dtype))
    keys = keys.reshape(2, nk * nh, nh * half).astype(BF16)
    return pl.pallas_call(
        _peer_scores_kernel,
        grid=(T // tm,),
        in_specs=[pl.BlockSpec((tm, D), lambda i: (i, 0)),
                  pl.BlockSpec((1, D), lambda i: (0, 0)),
                  pl.BlockSpec((D, 2 * nh * half), lambda i: (0, 0)),
                  pl.BlockSpec((2, nk * nh, nh * half), lambda i: (0, 0, 0))],
        out_specs=pl.BlockSpec((2, nk, nh, tm), lambda i: (0, 0, 0, i)),
        out_shape=jax.ShapeDtypeStruct((2, nk, nh, T), F32),
        compiler_params=_params("parallel"),
        name="peer_scores",
    )(x, g.reshape(1, D), wq, keys)


def _first_argmax(xs):
    nodes = [(xs[i], float(i)) for i in range(len(xs))]
    while len(nodes) > 1:
        merged = []
        for j in range(0, len(nodes) - 1, 2):
            (va, ia), (vb, ib) = nodes[j], nodes[j + 1]
            merged.append((jnp.maximum(va, vb), jnp.where(va >= vb, ia, ib)))
        if len(nodes) % 2:
            merged.append(nodes[-1])
        nodes = merged
    return nodes[0]


def _peer_route_kernel(s_ref, rank_ref, e2_ref, cnt_ref, c1_ref, work_ref, top_ref, arg_ref):
    nk = s_ref.shape[1]
    shape = s_ref.shape[1:]
    key_iota = lax.broadcasted_iota(jnp.int32, shape, 0).astype(F32)

    for side in range(2):
        work_ref[...] = s_ref[side]

        def extract(r, _, side=side):
            s = work_ref[...]
            m, idx = _first_argmax(s)
            work_ref[...] = jnp.where(key_iota == idx[None], -jnp.inf, s)
            top_ref[side, r] = m
            arg_ref[side, r] = idx
            return 0

        lax.fori_loop(0, PEER_TOPK, extract, 0)

    t0 = [top_ref[0, a] for a in range(PEER_TOPK)]
    t1 = [top_ref[1, b] for b in range(PEER_TOPK)]
    cand = [t0[a] + t1[b] for a, b in PEER_CELLS]
    best = cand[0]
    live = list(cand)
    chosen = [None] * len(PEER_CELLS)
    for _ in range(PEER_TOPK):
        _, first = _first_argmax(live)
        for ci in range(len(PEER_CELLS)):
            hit = first == float(ci)
            chosen[ci] = hit if chosen[ci] is None else chosen[ci] | hit
            live[ci] = jnp.where(hit, -jnp.inf, live[ci])
    denom = None
    cnt_a = [jnp.zeros(best.shape, F32) for _ in range(PEER_TOPK)]
    for ci, (a, b) in enumerate(PEER_CELLS):
        term = jnp.where(chosen[ci], jnp.exp(cand[ci] - best), 0.0)
        denom = term if denom is None else denom + term
        cnt_a[a] = cnt_a[a] + jnp.where(chosen[ci], 1.0, 0.0)
    inv_denom = 1.0 / denom

    cnt = jnp.zeros(shape, F32)
    rank = jnp.full(shape, float(PEER_TOPK), F32)
    for r in range(PEER_TOPK):
        cnt = jnp.where(key_iota == arg_ref[0, r][None], cnt_a[r][None], cnt)
        rank = jnp.where(key_iota == arg_ref[1, r][None], float(r), rank)
    cnt_ref[:, 0] = cnt
    c1_ref[:, 0] = jnp.exp(s_ref[0] - t0[0][None]) * (0.5 * inv_denom)[None]
    rank_ref[...] = rank
    e2_ref[...] = jnp.exp(s_ref[1] - t1[0][None])


def peer_route(scores):
    _, nk, nh, T = scores.shape
    tb = LANES
    spec = pl.BlockSpec((nk, nh, tb), lambda i: (0, 0, i))
    out = jax.ShapeDtypeStruct((nk, nh, T), F32)
    row_spec = pl.BlockSpec((nk, 1, nh, tb), lambda i: (0, i, 0, 0))
    row_out = jax.ShapeDtypeStruct((nk, T // tb, nh, tb), F32)
    return pl.pallas_call(
        _peer_route_kernel,
        grid=(T // tb,),
        in_specs=[pl.BlockSpec((2, nk, nh, tb), lambda i: (0, 0, 0, i))],
        out_specs=[spec, spec, row_spec, row_spec],
        out_shape=[out, out, row_out, row_out],
        scratch_shapes=[pltpu.VMEM((nk, nh, tb), F32),
                        pltpu.VMEM((2, PEER_TOPK, nh, tb), F32),
                        pltpu.VMEM((2, PEER_TOPK, nh, tb), F32)],
        compiler_params=_params("parallel"),
        name="peer_route",
    )(scores)


def _row_tile(ref, tile, hd, rows):
    parts = [ref[tile, q, pl.ds(hd, rows, stride=0), :] for q in range(ref.shape[1])]
    return jnp.concatenate(parts, axis=-1).astype(BF16)


def _peer_dense_kernel(x_ref, g_ref, u_ref, vt_ref, rank_ref, e2_ref, cnt_ref, c1_ref, o_ref, xnt_ref, acc_ref, *,
                       chunk):
    e = pl.program_id(1)

    @pl.when(e == 0)
    def _():
        xn = _rmsnorm_rows(x_ref[...], g_ref[...])
        xnt_ref[...] = xn.T.astype(BF16)
        acc_ref[...] = jnp.zeros_like(acc_ref)

    nh, groups, rows, tb = rank_ref.shape
    nk = groups * rows
    zero = jnp.zeros((groups, rows, tb), BF16)
    xnt = xnt_ref[...]
    total = None
    for c in range(u_ref.shape[0] // chunk):
        h = jnp.dot(u_ref[c * chunk:(c + 1) * chunk, :], xnt, preferred_element_type=F32)
        act = (h * (1.0 + lax.erf(h * (2.0 ** -0.5)))).astype(BF16)
        n_tiles = chunk // nk
        act = act.reshape(n_tiles, groups, rows, tb)
        w = [None] * n_tiles
        for hd in range(nh):
            rank = rank_ref[hd]
            e2 = e2_ref[hd]
            for ii in range(n_tiles):
                tile = c * n_tiles + ii
                cnt = _row_tile(cnt_ref, tile, hd, rows)
                c1 = _row_tile(c1_ref, tile, hd, rows)
                term = jnp.where(rank < cnt[None], e2 * c1[None], zero)
                w[ii] = term if w[ii] is None else w[ii] + term
        a = jnp.stack([act[ii] * w[ii] for ii in range(n_tiles)], axis=0).reshape(chunk, tb)
        part = jnp.dot(vt_ref[:, c * chunk:(c + 1) * chunk], a, preferred_element_type=F32)
        total = part if total is None else total + part
    acc_ref[...] += total

    @pl.when(e == pl.num_programs(1) - 1)
    def _():
        o_ref[...] = x_ref[...] + acc_ref[...].T


def peer_dense(x, g, u, vt, rank, e2, cnt, c1, *, tb, te, chunk):
    T, D = x.shape
    E = u.shape[0]
    nh, nk, _ = rank.shape
    rank = rank.reshape(nh, nk // BF16_ROWS, BF16_ROWS, T)
    e2 = e2.reshape(nh, nk // BF16_ROWS, BF16_ROWS, T)
    tab_spec = pl.BlockSpec((nh, nk // BF16_ROWS, BF16_ROWS, tb), lambda i, e: (0, 0, 0, i))
    row_spec = pl.BlockSpec((te // nk, tb // LANES, nh, LANES), lambda i, e: (e, i, 0, 0))
    return pl.pallas_call(
        functools.partial(_peer_dense_kernel, chunk=chunk),
        grid=(T // tb, E // te),
        in_specs=[pl.BlockSpec((tb, D), lambda i, e: (i, 0)),
                  pl.BlockSpec((1, D), lambda i, e: (0, 0)),
                  pl.BlockSpec((te, D), lambda i, e: (e, 0)),
                  pl.BlockSpec((D, te), lambda i, e: (0, e)),
                  tab_spec, tab_spec, row_spec, row_spec],
        out_specs=pl.BlockSpec((tb, D), lambda i, e: (i, 0)),
        out_shape=jax.ShapeDtypeStruct((T, D), F32),
        scratch_shapes=[pltpu.VMEM((D, tb), BF16), pltpu.VMEM((D, tb), F32)],
        compiler_params=_params("parallel", "arbitrary"),
        name="peer_dense",
    )(x, g.reshape(1, D), u, vt, rank, e2, cnt, c1)


def _expert_tables_kernel(u_ref, v_ref, ub_ref, vt_ref):
    ub_ref[...] = u_ref[0].astype(ub_ref.dtype)
    vt_ref[...] = v_ref[0].T.astype(vt_ref.dtype)


def expert_tables(peer_u, peer_v, layer, *, te):
    _, E, D = peer_u.shape
    in_spec = pl.BlockSpec((1, te, D), lambda e: (layer, e, 0))
    return pl.pallas_call(
        _expert_tables_kernel,
        grid=(E // te,),
        in_specs=[in_spec, in_spec],
        out_specs=[pl.BlockSpec((te, D), lambda e: (e, 0)), pl.BlockSpec((D, te), lambda e: (0, e))],
        out_shape=[jax.ShapeDtypeStruct((E, D), BF16), jax.ShapeDtypeStruct((D, E), BF16)],
        compiler_params=_params("parallel"),
        name="expert_tables",
    )(peer_u, peer_v)


def _final_norm_kernel(x_ref, g_ref, o_ref):
    o_ref[...] = _rmsnorm_rows(x_ref[...], g_ref[...])


def final_norm(x, g, *, tm):
    T, D = x.shape
    return pl.pallas_call(
        _final_norm_kernel,
        grid=(T // tm,),
        in_specs=[pl.BlockSpec((tm, D), lambda i: (i, 0)), pl.BlockSpec((1, D), lambda i: (0, 0))],
        out_specs=pl.BlockSpec((tm, D), lambda i: (i, 0)),
        out_shape=jax.ShapeDtypeStruct((T, D), F32),
        compiler_params=_params("parallel"),
        name="final_norm",
    )(x, g.reshape(1, D))


def _tile(n, want):
    if n <= want:
        return n
    t = want - want % LANES
    while n % t:
        t -= LANES
    return t


def kernel(x, mem, positions, g_mix, w_in, lam_qk, subln_g, w_pool, pool_scale, sg_ln_g, sg_ln_b, w_spatial,
           b_spatial, w_branch, w_out, g_cross, g_mem, w_xq, w_xkv, w_xo, g_ffn, w_pq, sub_keys, peer_u, peer_v,
           g_final):
    B, S, D = x.shape
    T = B * S
    depth = w_in.shape[0]
    x = x.reshape(T, D)
    memf = mem.reshape(-1, D)
    cos_tab, sin_tab, perm = rope_tables(positions)

    tm = _tile(S, 1024)
    tb_dense = _tile(S, 1024)

    for l in range(depth):
        lam_init = 0.8 - 0.6 * math.exp(-0.3 * l)
        lq = lam_qk[l].astype(F32)
        lam = (jnp.exp(jnp.sum(lq[0] * lq[1])) - jnp.exp(jnp.sum(lq[2] * lq[3])) + lam_init).reshape(1)

        z = norm_matmul(x, g_mix[l], w_in[l].astype(BF16), tm=tm, tn=1536, name="in_proj")
        qk = rope_qk(z, cos_tab, sin_tab, perm, tm=tm)
        a_out = diff_attention(qk, z, lam, subln_g[l], batch=B, seq=S, tq=_tile(S, 1024), tk=_tile(S, 512),
                               lam_init=lam_init)
        p_out = multiscale_pool(z, w_pool[l], pool_scale[l], batch=B, seq=S, tc=_tile(S, 256))
        s_out = chunk_spatial_gate(z, sg_ln_g[l], sg_ln_b[l], w_spatial[l], b_spatial[l], tc=_tile(S, 512))
        x = branch_merge(a_out, p_out, s_out, z, x, w_branch[l], w_out[l], tm=_tile(S, 512))

        kv = norm_matmul(memf, g_mem[l], w_xkv[l].astype(BF16), tm=_tile(memf.shape[0], 1024), tn=1024,
                         name="mem_kv")
        x = cross_attention(x, g_cross[l], kv, w_xq[l], w_xo[l], batch=B, seq=S, tm=_tile(S, 512))

        scores = peer_scores(x, g_ffn[l], w_pq[l], sub_keys[l], tm=_tile(S, 512))
        rank, e2, cnt, c1 = peer_route(scores)
        u_bf, vt_bf = expert_tables(peer_u, peer_v, l, te=512)
        x = peer_dense(x, g_ffn[l], u_bf, vt_bf,
                       rank.transpose(1, 0, 2).astype(BF16), e2.transpose(1, 0, 2).astype(BF16), cnt, c1,
                       tb=tb_dense, te=1024, chunk=1024)

    return final_norm(x, g_final, tm=tm).reshape(B, S, D)
```

```python
import functools
import math

import jax
import jax.numpy as jnp
from jax import lax
from jax.experimental import pallas as pl
from jax.experimental.pallas import tpu as pltpu

F32 = jnp.float32
BF16 = jnp.bfloat16

EPS = 1e-6
LANES = 128
BF16_ROWS = 16
VMEM_LIMIT_BYTES = 48 * 1024 * 1024

DA_HEADS = 4
DA_HEAD_DIM = 64
ROT_DIM = DA_HEAD_DIM // 4
ROPE_THETA = 500000.0
POOL_WINDOWS = (2, 4, 8, 16)
CHUNK = 128
SG_GROUPS = 4
N_BRANCH = 3
X_HEADS = 4
PEER_TOPK = 16
PEER_CELLS = tuple((a, b) for a in range(PEER_TOPK) for b in range(PEER_TOPK) if (a + 1) * (b + 1) <= PEER_TOPK)


def _params(*sem):
    return pltpu.CompilerParams(dimension_semantics=sem, vmem_limit_bytes=VMEM_LIMIT_BYTES)


def _rmsnorm_rows(x, g):
    ms = jnp.mean(x * x, axis=-1, keepdims=True)
    return x * lax.rsqrt(ms + EPS) * g


def _norm_matmul_kernel(x_ref, g_ref, w_ref, o_ref, xn_ref):
    @pl.when(pl.program_id(1) == 0)
    def _():
        xn_ref[...] = _rmsnorm_rows(x_ref[...], g_ref[...]).astype(BF16)

    o_ref[...] = jnp.dot(xn_ref[...], w_ref[...], preferred_element_type=F32).astype(o_ref.dtype)


def norm_matmul(x, g, w, *, tm, tn, name):
    T, D = x.shape
    N = w.shape[1]
    return pl.pallas_call(
        _norm_matmul_kernel,
        grid=(T // tm, N // tn),
        in_specs=[pl.BlockSpec((tm, D), lambda i, j: (i, 0)),
                  pl.BlockSpec((1, D), lambda i, j: (0, 0)),
                  pl.BlockSpec((D, tn), lambda i, j: (0, j))],
        out_specs=pl.BlockSpec((tm, tn), lambda i, j: (i, j)),
        out_shape=jax.ShapeDtypeStruct((T, N), BF16),
        scratch_shapes=[pltpu.VMEM((tm, D), BF16)],
        compiler_params=_params("parallel", "arbitrary"),
        name=name,
    )(x, g.reshape(1, D), w)


def _rope_kernel(z_ref, c_ref, s_ref, p_ref, o_ref, *, n_q_groups):
    c = c_ref[...]
    s = s_ref[...]
    for gi in range(z_ref.shape[1] // LANES):
        cols = slice(gi * LANES, (gi + 1) * LANES)
        xg = z_ref[:, cols]
        swapped = jnp.dot(xg, p_ref[...], preferred_element_type=F32)
        r = xg.astype(F32) * c + swapped * s
        if gi < n_q_groups:
            r = r * (DA_HEAD_DIM ** -0.5)
        o_ref[:, cols] = r.astype(o_ref.dtype)


def rope_qk(z, cos_tab, sin_tab, perm, *, tm):
    T = z.shape[0]
    width = 2 * DA_HEADS * 2 * DA_HEAD_DIM
    return pl.pallas_call(
        functools.partial(_rope_kernel, n_q_groups=width // (2 * LANES)),
        grid=(T // tm,),
        in_specs=[pl.BlockSpec((tm, width), lambda i: (i, 0)),
                  pl.BlockSpec((tm, LANES), lambda i: (i, 0)),
                  pl.BlockSpec((tm, LANES), lambda i: (i, 0)),
                  pl.BlockSpec((LANES, LANES), lambda i: (0, 0))],
        out_specs=pl.BlockSpec((tm, width), lambda i: (i, 0)),
        out_shape=jax.ShapeDtypeStruct((T, width), BF16),
        compiler_params=_params("parallel"),
        name="rope_qk",
    )(z, cos_tab, sin_tab, perm)


def rope_tables(positions):
    half = ROT_DIM // 2
    inv = ROPE_THETA ** (-jnp.arange(0, ROT_DIM, 2, dtype=F32) / ROT_DIM)
    ang = positions.astype(F32).reshape(-1, 1) * inv
    cos, sin = jnp.cos(ang), jnp.sin(ang)
    T = ang.shape[0]
    ones = jnp.ones((T, DA_HEAD_DIM - ROT_DIM), F32)
    zeros = jnp.zeros((T, DA_HEAD_DIM - ROT_DIM), F32)
    c64 = jnp.concatenate([cos, cos, ones], axis=1)
    s64 = jnp.concatenate([-sin, sin, zeros], axis=1)
    lane = jnp.arange(LANES)
    d = lane % DA_HEAD_DIM
    src = jnp.where(d < half, lane + half, jnp.where(d < ROT_DIM, lane - half, -1))
    perm = (lane[:, None] == src[None, :]).astype(BF16)
    return jnp.tile(c64, (1, 2)), jnp.tile(s64, (1, 2)), perm


def _diff_attn_kernel(lam_ref, q_ref, k_ref, v_ref, g_ref, o_ref, *, tq, tk, lam_init):
    qi = pl.program_id(2)
    q = q_ref[...]
    lane = lax.broadcasted_iota(jnp.int32, q.shape, 1)
    zero = jnp.zeros_like(q)
    q_maps = (jnp.where(lane < DA_HEAD_DIM, q, zero), jnp.where(lane >= DA_HEAD_DIM, q, zero))
    row = lax.broadcasted_iota(jnp.int32, (tq, tk), 0)
    col = lax.broadcasted_iota(jnp.int32, (tq, tk), 1)
    per_q = tq // tk

    def block(kb, carry, diag_offset):
        start = pl.multiple_of(kb * tk, tk)
        k = k_ref[pl.ds(start, tk), :]
        v = v_ref[pl.ds(start, tk), :]
        new = []
        for mi in range(2):
            m, l, acc = carry[3 * mi:3 * mi + 3]
            s = lax.dot_general(q_maps[mi], k, (((1,), (1,)), ((), ())), preferred_element_type=F32)
            if diag_offset is not None:
                s = jnp.where(col + diag_offset <= row, s, -jnp.inf)
            m_new = jnp.maximum(m, jnp.max(s, axis=1, keepdims=True))
            p = jnp.exp(s - m_new)
            alpha = jnp.exp(m - m_new)
            l = alpha * l + jnp.sum(p, axis=1, keepdims=True)
            acc = alpha * acc + jnp.dot(p.astype(BF16), v, preferred_element_type=F32)
            new += [m_new, l, acc]
        return tuple(new)

    init = (jnp.full((tq, 1), -jnp.inf, F32), jnp.zeros((tq, 1), F32), jnp.zeros((tq, LANES), F32)) * 2
    carry = lax.fori_loop(0, qi * per_q, lambda kb, c: block(kb, c, None), init)
    for d in range(per_q):
        carry = block(qi * per_q + d, carry, d * tk)
    m1, l1, a1, m2, l2, a2 = carry
    o = a1 / l1 - lam_ref[0] * (a2 / l2)
    o_ref[...] = (_rmsnorm_rows(o, g_ref[...]) * (1.0 - lam_init)).astype(o_ref.dtype)


def diff_attention(qk, z, lam, subln_g, *, batch, seq, tq, tk, lam_init):
    T = qk.shape[0]
    nq = seq // tq
    width = DA_HEADS * 2 * DA_HEAD_DIM
    return pl.pallas_call(
        functools.partial(_diff_attn_kernel, tq=tq, tk=tk, lam_init=lam_init),
        grid=(batch, DA_HEADS, nq),
        in_specs=[pl.BlockSpec(memory_space=pltpu.SMEM),
                  pl.BlockSpec((tq, LANES), lambda b, h, i: (b * nq + i, h)),
                  pl.BlockSpec((seq, LANES), lambda b, h, i: (b, DA_HEADS + h)),
                  pl.BlockSpec((seq, LANES), lambda b, h, i: (b, 2 * DA_HEADS + h)),
                  pl.BlockSpec((1, LANES), lambda b, h, i: (0, 0))],
        out_specs=pl.BlockSpec((tq, LANES), lambda b, h, i: (b * nq + i, h)),
        out_shape=jax.ShapeDtypeStruct((T, width), BF16),
        compiler_params=_params("parallel", "parallel", "arbitrary"),
        name="diff_attention",
    )(lam, qk, qk, z, subln_g.reshape(1, LANES))


def _pool_kernel(p_ref, prev_ref, cur_band_ref, prev_band_ref, inv_ref, w_ref, scale_ref, o_ref):
    has_prev = pl.program_id(1) > 0
    for g in range(len(POOL_WINDOWS)):
        cols = slice(g * LANES, (g + 1) * LANES)
        pg = p_ref[:, cols]
        wsum = jnp.dot(cur_band_ref[g], pg, preferred_element_type=F32)
        halo = jnp.dot(prev_band_ref[g], prev_ref[:, cols], preferred_element_type=F32)
        wsum = wsum + jnp.where(has_prev, halo, 0.0)
        d = wsum * inv_ref[:, cols] - pg.astype(F32)
        y = jnp.dot(d.astype(BF16), w_ref[g], preferred_element_type=F32) * scale_ref[:, cols]
        o_ref[:, cols] = y.astype(o_ref.dtype)


def multiscale_pool(z, w_pool, pool_scale, *, batch, seq, tc):
    T = z.shape[0]
    nc = seq // tc
    width = len(POOL_WINDOWS) * LANES
    col_block = 3
    t = jnp.arange(tc)
    delta = t[:, None] - t[None, :]
    win = jnp.asarray(POOL_WINDOWS)[:, None, None]
    cur_band = ((delta >= 0) & (delta < win)).astype(BF16)
    prev_band = ((delta + tc >= 0) & (delta + tc < win)).astype(BF16)
    pos = jnp.arange(seq)
    inv_cnt = 1.0 / jnp.minimum(pos[:, None] + 1, jnp.repeat(jnp.asarray(POOL_WINDOWS), LANES)[None, :]).astype(F32)
    return pl.pallas_call(
        _pool_kernel,
        grid=(batch, nc),
        in_specs=[pl.BlockSpec((tc, width), lambda b, c: (b * nc + c, col_block)),
                  pl.BlockSpec((tc, width), lambda b, c: (b * nc + jnp.maximum(c - 1, 0), col_block)),
                  pl.BlockSpec((len(POOL_WINDOWS), tc, tc), lambda b, c: (0, 0, 0)),
                  pl.BlockSpec((len(POOL_WINDOWS), tc, tc), lambda b, c: (0, 0, 0)),
                  pl.BlockSpec((tc, width), lambda b, c: (c, 0)),
                  pl.BlockSpec((len(POOL_WINDOWS), LANES, LANES), lambda b, c: (0, 0, 0)),
                  pl.BlockSpec((1, width), lambda b, c: (0, 0))],
        out_specs=pl.BlockSpec((tc, width), lambda b, c: (b * nc + c, 0)),
        out_shape=jax.ShapeDtypeStruct((T, width), BF16),
        compiler_params=_params("parallel", "arbitrary"),
        name="multiscale_pool",
    )(z, z, cur_band, prev_band, inv_cnt, w_pool.astype(BF16), pool_scale.reshape(1, width))


def _sgate_kernel(u_ref, v_ref, g_ref, b_ref, ws_ref, bias_ref, o_ref):
    v = v_ref[...].astype(F32)
    mu = jnp.mean(v, axis=-1, keepdims=True)
    vc = v - mu
    var = jnp.mean(vc * vc, axis=-1, keepdims=True)
    vn = (vc * lax.rsqrt(var + EPS) * g_ref[...] + b_ref[...]).astype(BF16)
    for n in range(v.shape[0] // CHUNK):
        rows = slice(n * CHUNK, (n + 1) * CHUNK)
        for g in range(SG_GROUPS):
            cols = slice(g * LANES, (g + 1) * LANES)
            mixed = jnp.dot(ws_ref[g], vn[rows, cols], preferred_element_type=F32) + bias_ref[:, cols]
            o_ref[rows, cols] = (u_ref[rows, cols].astype(F32) * mixed).astype(o_ref.dtype)


def chunk_spatial_gate(z, ln_g, ln_b, w_s, b_s, *, tc):
    T = z.shape[0]
    width = SG_GROUPS * LANES
    ws = (w_s * jnp.tril(jnp.ones((CHUNK, CHUNK), w_s.dtype))).astype(BF16)
    bias = jnp.repeat(b_s.T, LANES, axis=1)
    return pl.pallas_call(
        _sgate_kernel,
        grid=(T // tc,),
        in_specs=[pl.BlockSpec((tc, width), lambda i: (i, 4)),
                  pl.BlockSpec((tc, width), lambda i: (i, 5)),
                  pl.BlockSpec((1, width), lambda i: (0, 0)),
                  pl.BlockSpec((1, width), lambda i: (0, 0)),
                  pl.BlockSpec((SG_GROUPS, CHUNK, CHUNK), lambda i: (0, 0, 0)),
                  pl.BlockSpec((CHUNK, width), lambda i: (0, 0))],
        out_specs=pl.BlockSpec((tc, width), lambda i: (i, 0)),
        out_shape=jax.ShapeDtypeStruct((T, width), BF16),
        compiler_params=_params("parallel"),
        name="chunk_spatial_gate",
    )(z, z, ln_g.reshape(1, width), ln_b.reshape(1, width), ws, bias)


def _merge_kernel(a_ref, p_ref, s_ref, gate_ref, x_ref, wb_ref, wo_ref, o_ref):
    d = x_ref.shape[1]
    merged = None
    for n, br in enumerate((a_ref, p_ref, s_ref)):
        proj = jnp.dot(br[...], wb_ref[n], preferred_element_type=F32)
        gate = jax.nn.sigmoid(gate_ref[:, n * d:(n + 1) * d].astype(F32))
        merged = gate * proj if merged is None else merged + gate * proj
    o_ref[...] = x_ref[...] + jnp.dot(merged.astype(BF16), wo_ref[...], preferred_element_type=F32)


def branch_merge(a_out, p_out, s_out, z, x, w_branch, w_out, *, tm):
    T, D = x.shape
    bw = a_out.shape[1]
    br_spec = pl.BlockSpec((tm, bw), lambda i: (i, 0))
    return pl.pallas_call(
        _merge_kernel,
        grid=(T // tm,),
        in_specs=[br_spec, br_spec, br_spec,
                  pl.BlockSpec((tm, N_BRANCH * D), lambda i: (i, 1)),
                  pl.BlockSpec((tm, D), lambda i: (i, 0)),
                  pl.BlockSpec((N_BRANCH, bw, D), lambda i: (0, 0, 0)),
                  pl.BlockSpec((D, D), lambda i: (0, 0))],
        out_specs=pl.BlockSpec((tm, D), lambda i: (i, 0)),
        out_shape=jax.ShapeDtypeStruct((T, D), F32),
        compiler_params=_params("parallel"),
        name="branch_merge",
    )(a_out, p_out, s_out, z, x, w_branch.astype(BF16), w_out.astype(BF16))


def _cross_kernel(x_ref, g_ref, k_ref, v_ref, wq_ref, wo_ref, o_ref):
    x = x_ref[...]
    d = x.shape[1]
    hd = d // X_HEADS
    xn = _rmsnorm_rows(x, g_ref[...]).astype(BF16)
    q = jnp.dot(xn, wq_ref[...], preferred_element_type=F32).astype(BF16)
    heads = []
    for h in range(X_HEADS):
        cols = slice(h * hd, (h + 1) * hd)
        s = lax.dot_general(q[:, cols], k_ref[:, cols], (((1,), (1,)), ((), ())), preferred_element_type=F32)
        s = s * (hd ** -0.5)
        p = jnp.exp(s - jnp.max(s, axis=1, keepdims=True))
        p = p / jnp.sum(p, axis=1, keepdims=True)
        heads.append(jnp.dot(p.astype(BF16), v_ref[:, cols], preferred_element_type=F32).astype(BF16))
    o = jnp.concatenate(heads, axis=1)
    o_ref[...] = x + jnp.dot(o, wo_ref[...], preferred_element_type=F32)


def cross_attention(x, g, kv, w_q, w_o, *, batch, seq, tm):
    T, D = x.shape
    M = kv.shape[0] // batch
    nb = seq // tm
    return pl.pallas_call(
        _cross_kernel,
        grid=(batch, nb),
        in_specs=[pl.BlockSpec((tm, D), lambda b, i: (b * nb + i, 0)),
                  pl.BlockSpec((1, D), lambda b, i: (0, 0)),
                  pl.BlockSpec((M, D), lambda b, i: (b, 0)),
                  pl.BlockSpec((M, D), lambda b, i: (b, 1)),
                  pl.BlockSpec((D, D), lambda b, i: (0, 0)),
                  pl.BlockSpec((D, D), lambda b, i: (0, 0))],
        out_specs=pl.BlockSpec((tm, D), lambda b, i: (b * nb + i, 0)),
        out_shape=jax.ShapeDtypeStruct((T, D), F32),
        compiler_params=_params("parallel", "parallel"),
        name="cross_attention",
    )(x, g.reshape(1, D), kv, kv, w_q.astype(BF16), w_o.astype(BF16))


def _peer_scores_kernel(x_ref, g_ref, wq_ref, keys_ref, o_ref):
    xn = _rmsnorm_rows(x_ref[...], g_ref[...]).astype(BF16)
    q = jnp.dot(xn, wq_ref[...], preferred_element_type=F32).astype(BF16)
    _, nk, nh, tm = o_ref.shape
    width = keys_ref.shape[2]
    for side in range(2):
        st = lax.dot_general(keys_ref[side], q[:, side * width:(side + 1) * width],
                             (((1,), (1,)), ((), ())), preferred_element_type=F32)
        o_ref[side] = st.reshape(nk, nh, tm)


def peer_scores(x, g, w_pq, sub_keys, *, tm):
    T, D = x.shape
    nh, _, nk, half = sub_keys.shape
    wq = w_pq.reshape(D, nh, 2, half).transpose(0, 2, 1, 3).reshape(D, 2 * nh * half).astype(BF16)
    keys = jnp.einsum('hsnk,hg->snhgk', sub_keys, jnp.eye(nh, dtype=sub_keys.dtype))
    keys = keys.reshape(2, nk * nh, nh * half).astype(BF16)
    return pl.pallas_call(
        _peer_scores_kernel,
        grid=(T // tm,),
        in_specs=[pl.BlockSpec((tm, D), lambda i: (i, 0)),
                  pl.BlockSpec((1, D), lambda i: (0, 0)),
                  pl.BlockSpec((D, 2 * nh * half), lambda i: (0, 0)),
                  pl.BlockSpec((2, nk * nh, nh * half), lambda i: (0, 0, 0))],
        out_specs=pl.BlockSpec((2, nk, nh, tm), lambda i: (0, 0, 0, i)),
        out_shape=jax.ShapeDtypeStruct((2, nk, nh, T), F32),
        compiler_params=_params("parallel"),
        name="peer_scores",
    )(x, g.reshape(1, D), wq, keys)


def _first_argmax(xs):
    nodes = [(xs[i], float(i)) for i in range(len(xs))]
    while len(nodes) > 1:
        merged = []
        for j in range(0, len(nodes) - 1, 2):
            (va, ia), (vb, ib) = nodes[j], nodes[j + 1]
            merged.append((jnp.maximum(va, vb), jnp.where(va >= vb, ia, ib)))
        if len(nodes) % 2:
            merged.append(nodes[-1])
        nodes = merged
    return nodes[0]


def _peer_route_kernel(s_ref, rank_ref, e2_ref, cnt_ref, c1_ref, work_ref, top_ref, arg_ref):
    nk = s_ref.shape[1]
    shape = s_ref.shape[1:]
    key_iota = lax.broadcasted_iota(jnp.int32, shape, 0).astype(F32)

    for side in range(2):
        work_ref[...] = s_ref[side]

        def extract(r, _, side=side):
            s = work_ref[...]
            m, idx = _first_argmax(s)
            work_ref[...] = jnp.where(key_iota == idx[None], -jnp.inf, s)
            top_ref[side, r] = m
            arg_ref[side, r] = idx
            return 0

        lax.fori_loop(0, PEER_TOPK, extract, 0)

    t0 = [top_ref[0, a] for a in range(PEER_TOPK)]
    t1 = [top_ref[1, b] for b in range(PEER_TOPK)]
    cand = [t0[a] + t1[b] for a, b in PEER_CELLS]
    best = cand[0]
    live = list(cand)
    chosen = [None] * len(PEER_CELLS)
    for _ in range(PEER_TOPK):
        _, first = _first_argmax(live)
        for ci in range(len(PEER_CELLS)):
            hit = first == float(ci)
            chosen[ci] = hit if chosen[ci] is None else chosen[ci] | hit
            live[ci] = jnp.where(hit, -jnp.inf, live[ci])
    denom = None
    cnt_a = [jnp.zeros(best.shape, F32) for _ in range(PEER_TOPK)]
    for ci, (a, b) in enumerate(PEER_CELLS):
        term = jnp.where(chosen[ci], jnp.exp(cand[ci] - best), 0.0)
        denom = term if denom is None else denom + term
        cnt_a[a] = cnt_a[a] + jnp.where(chosen[ci], 1.0, 0.0)
    inv_denom = 1.0 / denom

    cnt = jnp.zeros(shape, F32)
    rank = jnp.full(shape, float(PEER_TOPK), F32)
    for r in range(PEER_TOPK):
        cnt = jnp.where(key_iota == arg_ref[0, r][None], cnt_a[r][None], cnt)
        rank = jnp.where(key_iota == arg_ref[1, r][None], float(r), rank)
    cnt_ref[:, 0] = cnt
    c1_ref[:, 0] = jnp.exp(s_ref[0] - t0[0][None]) * (0.5 * inv_denom)[None]
    rank_ref[...] = rank
    e2_ref[...] = jnp.exp(s_ref[1] - t1[0][None])


def peer_route(scores):
    _, nk, nh, T = scores.shape
    tb = LANES
    spec = pl.BlockSpec((nk, nh, tb), lambda i: (0, 0, i))
    out = jax.ShapeDtypeStruct((nk, nh, T), F32)
    row_spec = pl.BlockSpec((nk, 1, nh, tb), lambda i: (0, i, 0, 0))
    row_out = jax.ShapeDtypeStruct((nk, T // tb, nh, tb), F32)
    return pl.pallas_call(
        _peer_route_kernel,
        grid=(T // tb,),
        in_specs=[pl.BlockSpec((2, nk, nh, tb), lambda i: (0, 0, 0, i))],
        out_specs=[spec, spec, row_spec, row_spec],
        out_shape=[out, out, row_out, row_out],
        scratch_shapes=[pltpu.VMEM((nk, nh, tb), F32),
                        pltpu.VMEM((2, PEER_TOPK, nh, tb), F32),
                        pltpu.VMEM((2, PEER_TOPK, nh, tb), F32)],
        compiler_params=_params("parallel"),
        name="peer_route",
    )(scores)


def _row_tile(ref, tile, hd, rows):
    parts = [ref[tile, q, pl.ds(hd, rows, stride=0), :] for q in range(ref.shape[1])]
    return jnp.concatenate(parts, axis=-1).astype(BF16)


def _peer_dense_kernel(x_ref, g_ref, u_ref, vt_ref, rank_ref, e2_ref, cnt_ref, c1_ref, o_ref, xnt_ref, acc_ref, *,
                       chunk):
    e = pl.program_id(1)

    @pl.when(e == 0)
    def _():
        xn = _rmsnorm_rows(x_ref[...], g_ref[...])
        xnt_ref[...] = xn.T.astype(BF16)
        acc_ref[...] = jnp.zeros_like(acc_ref)

    nh, groups, rows, tb = rank_ref.shape
    nk = groups * rows
    zero = jnp.zeros((groups, rows, tb), BF16)
    xnt = xnt_ref[...]
    total = None
    for c in range(u_ref.shape[0] // chunk):
        h = jnp.dot(u_ref[c * chunk:(c + 1) * chunk, :], xnt, preferred_element_type=F32)
        hb = h.astype(BF16)
        act = hb * (1.0 + lax.erf(hb * (2.0 ** -0.5)))
        n_tiles = chunk // nk
        act = act.reshape(n_tiles, groups, rows, tb)
        w = [None] * n_tiles
        for hd in range(nh):
            rank = rank_ref[hd]
            e2 = e2_ref[hd]
            for ii in range(n_tiles):
                tile = c * n_tiles + ii
                cnt = _row_tile(cnt_ref, tile, hd, rows)
                c1 = _row_tile(c1_ref, tile, hd, rows)
                term = jnp.where(rank < cnt[None], e2 * c1[None], zero)
                w[ii] = term if w[ii] is None else w[ii] + term
        a = jnp.stack([act[ii] * w[ii] for ii in range(n_tiles)], axis=0).reshape(chunk, tb)
        part = jnp.dot(vt_ref[:, c * chunk:(c + 1) * chunk], a, preferred_element_type=F32)
        total = part if total is None else total + part
    acc_ref[...] += total

    @pl.when(e == pl.num_programs(1) - 1)
    def _():
        o_ref[...] = x_ref[...] + acc_ref[...].T


def peer_dense(x, g, u, vt, rank, e2, cnt, c1, *, tb, te, chunk):
    T, D = x.shape
    E = u.shape[0]
    nh, nk, _ = rank.shape
    rank = rank.reshape(nh, nk // BF16_ROWS, BF16_ROWS, T)
    e2 = e2.reshape(nh, nk // BF16_ROWS, BF16_ROWS, T)
    tab_spec = pl.BlockSpec((nh, nk // BF16_ROWS, BF16_ROWS, tb), lambda i, e: (0, 0, 0, i))
    row_spec = pl.BlockSpec((te // nk, tb // LANES, nh, LANES), lambda i, e: (e, i, 0, 0))
    return pl.pallas_call(
        functools.partial(_peer_dense_kernel, chunk=chunk),
        grid=(T // tb, E // te),
        in_specs=[pl.BlockSpec((tb, D), lambda i, e: (i, 0)),
                  pl.BlockSpec((1, D), lambda i, e: (0, 0)),
                  pl.BlockSpec((te, D), lambda i, e: (e, 0)),
                  pl.BlockSpec((D, te), lambda i, e: (0, e)),
                  tab_spec, tab_spec, row_spec, row_spec],
        out_specs=pl.BlockSpec((tb, D), lambda i, e: (i, 0)),
        out_shape=jax.ShapeDtypeStruct((T, D), F32),
        scratch_shapes=[pltpu.VMEM((D, tb), BF16), pltpu.VMEM((D, tb), F32)],
        compiler_params=_params("parallel", "arbitrary"),
        name="peer_dense",
    )(x, g.reshape(1, D), u, vt, rank, e2, cnt, c1)


def _expert_tables_kernel(u_ref, v_ref, ub_ref, vt_ref):
    ub_ref[...] = u_ref[0].astype(ub_ref.dtype)
    vt_ref[...] = v_ref[0].T.astype(vt_ref.dtype)


def expert_tables(peer_u, peer_v, layer, *, te):
    _, E, D = peer_u.shape
    in_spec = pl.BlockSpec((1, te, D), lambda e: (layer, e, 0))
    return pl.pallas_call(
        _expert_tables_kernel,
        grid=(E // te,),
        in_specs=[in_spec, in_spec],
        out_specs=[pl.BlockSpec((te, D), lambda e: (e, 0)), pl.BlockSpec((D, te), lambda e: (0, e))],
        out_shape=[jax.ShapeDtypeStruct((E, D), BF16), jax.ShapeDtypeStruct((D, E), BF16)],
        compiler_params=_params("parallel"),
        name="expert_tables",
    )(peer_u, peer_v)


def _final_norm_kernel(x_ref, g_ref, o_ref):
    o_ref[...] = _rmsnorm_rows(x_ref[...], g_ref[...])


def final_norm(x, g, *, tm):
    T, D = x.shape
    return pl.pallas_call(
        _final_norm_kernel,
        grid=(T // tm,),
        in_specs=[pl.BlockSpec((tm, D), lambda i: (i, 0)), pl.BlockSpec((1, D), lambda i: (0, 0))],
        out_specs=pl.BlockSpec((tm, D), lambda i: (i, 0)),
        out_shape=jax.ShapeDtypeStruct((T, D), F32),
        compiler_params=_params("parallel"),
        name="final_norm",
    )(x, g.reshape(1, D))


def _tile(n, want):
    if n <= want:
        return n
    t = want - want % LANES
    while n % t:
        t -= LANES
    return t


def kernel(x, mem, positions, g_mix, w_in, lam_qk, subln_g, w_pool, pool_scale, sg_ln_g, sg_ln_b, w_spatial,
           b_spatial, w_branch, w_out, g_cross, g_mem, w_xq, w_xkv, w_xo, g_ffn, w_pq, sub_keys, peer_u, peer_v,
           g_final):
    B, S, D = x.shape
    T = B * S
    depth = w_in.shape[0]
    x = x.reshape(T, D)
    memf = mem.reshape(-1, D)
    cos_tab, sin_tab, perm = rope_tables(positions)

    tm = _tile(S, 1024)
    tb_dense = _tile(S, 1024)

    for l in range(depth):
        lam_init = 0.8 - 0.6 * math.exp(-0.3 * l)
        lq = lam_qk[l].astype(F32)
        lam = (jnp.exp(jnp.sum(lq[0] * lq[1])) - jnp.exp(jnp.sum(lq[2] * lq[3])) + lam_init).reshape(1)

        z = norm_matmul(x, g_mix[l], w_in[l].astype(BF16), tm=tm, tn=1536, name="in_proj")
        qk = rope_qk(z, cos_tab, sin_tab, perm, tm=tm)
        a_out = diff_attention(qk, z, lam, subln_g[l], batch=B, seq=S, tq=_tile(S, 1024), tk=_tile(S, 512),
                               lam_init=lam_init)
        p_out = multiscale_pool(z, w_pool[l], pool_scale[l], batch=B, seq=S, tc=_tile(S, 256))
        s_out = chunk_spatial_gate(z, sg_ln_g[l], sg_ln_b[l], w_spatial[l], b_spatial[l], tc=_tile(S, 512))
        x = branch_merge(a_out, p_out, s_out, z, x, w_branch[l], w_out[l], tm=_tile(S, 512))

        kv = norm_matmul(memf, g_mem[l], w_xkv[l].astype(BF16), tm=_tile(memf.shape[0], 1024), tn=1024,
                         name="mem_kv")
        x = cross_attention(x, g_cross[l], kv, w_xq[l], w_xo[l], batch=B, seq=S, tm=_tile(S, 512))

        scores = peer_scores(x, g_ffn[l], w_pq[l], sub_keys[l], tm=_tile(S, 512))
        rank, e2, cnt, c1 = peer_route(scores)
        u_bf, vt_bf = expert_tables(peer_u, peer_v, l, te=512)
        x = peer_dense(x, g_ffn[l], u_bf, vt_bf,
                       rank.transpose(1, 0, 2).astype(BF16), e2.transpose(1, 0, 2).astype(BF16), cnt, c1,
                       tb=tb_dense, te=1024, chunk=1024)

    return final_norm(x, g_final, tm=tm).reshape(B, S, D)
```

```python
import functools
import math

import jax
import jax.numpy as jnp
from jax import lax
from jax.experimental import pallas as pl
from jax.experimental.pallas import tpu as pltpu

F32 = jnp.float32
BF16 = jnp.bfloat16

EPS = 1e-6
LANES = 128
BF16_ROWS = 16
VMEM_LIMIT_BYTES = 48 * 1024 * 1024

DA_HEADS = 4
DA_HEAD_DIM = 64
ROT_DIM = DA_HEAD_DIM // 4
ROPE_THETA = 500000.0
POOL_WINDOWS = (2, 4, 8, 16)
CHUNK = 128
SG_GROUPS = 4
N_BRANCH = 3
X_HEADS = 4
PEER_TOPK = 16
PEER_CELLS = tuple((a, b) for a in range(PEER_TOPK) for b in range(PEER_TOPK) if (a + 1) * (b + 1) <= PEER_TOPK)


def _params(*sem):
    return pltpu.CompilerParams(dimension_semantics=sem, vmem_limit_bytes=VMEM_LIMIT_BYTES)


def _rmsnorm_rows(x, g):
    ms = jnp.mean(x * x, axis=-1, keepdims=True)
    return x * lax.rsqrt(ms + EPS) * g


def _norm_matmul_kernel(x_ref, g_ref, w_ref, o_ref, xn_ref):
    @pl.when(pl.program_id(1) == 0)
    def _():
        xn_ref[...] = _rmsnorm_rows(x_ref[...], g_ref[...]).astype(BF16)

    o_ref[...] = jnp.dot(xn_ref[...], w_ref[...], preferred_element_type=F32).astype(o_ref.dtype)


def norm_matmul(x, g, w, *, tm, tn, name):
    T, D = x.shape
    N = w.shape[1]
    return pl.pallas_call(
        _norm_matmul_kernel,
        grid=(T // tm, N // tn),
        in_specs=[pl.BlockSpec((tm, D), lambda i, j: (i, 0)),
                  pl.BlockSpec((1, D), lambda i, j: (0, 0)),
                  pl.BlockSpec((D, tn), lambda i, j: (0, j))],
        out_specs=pl.BlockSpec((tm, tn), lambda i, j: (i, j)),
        out_shape=jax.ShapeDtypeStruct((T, N), BF16),
        scratch_shapes=[pltpu.VMEM((tm, D), BF16)],
        compiler_params=_params("parallel", "arbitrary"),
        name=name,
    )(x, g.reshape(1, D), w)


def _rope_kernel(z_ref, c_ref, s_ref, p_ref, o_ref, *, n_q_groups):
    c = c_ref[...]
    s = s_ref[...]
    for gi in range(z_ref.shape[1] // LANES):
        cols = slice(gi * LANES, (gi + 1) * LANES)
        xg = z_ref[:, cols]
        swapped = jnp.dot(xg, p_ref[...], preferred_element_type=F32)
        r = xg.astype(F32) * c + swapped * s
        if gi < n_q_groups:
            r = r * (DA_HEAD_DIM ** -0.5)
        o_ref[:, cols] = r.astype(o_ref.dtype)


def rope_qk(z, cos_tab, sin_tab, perm, *, tm):
    T = z.shape[0]
    width = 2 * DA_HEADS * 2 * DA_HEAD_DIM
    return pl.pallas_call(
        functools.partial(_rope_kernel, n_q_groups=width // (2 * LANES)),
        grid=(T // tm,),
        in_specs=[pl.BlockSpec((tm, width), lambda i: (i, 0)),
                  pl.BlockSpec((tm, LANES), lambda i: (i, 0)),
                  pl.BlockSpec((tm, LANES), lambda i: (i, 0)),
                  pl.BlockSpec((LANES, LANES), lambda i: (0, 0))],
        out_specs=pl.BlockSpec((tm, width), lambda i: (i, 0)),
        out_shape=jax.ShapeDtypeStruct((T, width), BF16),
        compiler_params=_params("parallel"),
        name="rope_qk",
    )(z, cos_tab, sin_tab, perm)


def rope_tables(positions):
    half = ROT_DIM // 2
    inv = ROPE_THETA ** (-jnp.arange(0, ROT_DIM, 2, dtype=F32) / ROT_DIM)
    ang = positions.astype(F32).reshape(-1, 1) * inv
    cos, sin = jnp.cos(ang), jnp.sin(ang)
    T = ang.shape[0]
    ones = jnp.ones((T, DA_HEAD_DIM - ROT_DIM), F32)
    zeros = jnp.zeros((T, DA_HEAD_DIM - ROT_DIM), F32)
    c64 = jnp.concatenate([cos, cos, ones], axis=1)
    s64 = jnp.concatenate([-sin, sin, zeros], axis=1)
    lane = jnp.arange(LANES)
    d = lane % DA_HEAD_DIM
    src = jnp.where(d < half, lane + half, jnp.where(d < ROT_DIM, lane - half, -1))
    perm = (lane[:, None] == src[None, :]).astype(BF16)
    return jnp.tile(c64, (1, 2)), jnp.tile(s64, (1, 2)), perm


def _diff_attn_kernel(lam_ref, q_ref, k_ref, v_ref, g_ref, o_ref, *, tq, tk, lam_init):
    qi = pl.program_id(2)
    q = q_ref[...]
    lane = lax.broadcasted_iota(jnp.int32, q.shape, 1)
    zero = jnp.zeros_like(q)
    q_maps = (jnp.where(lane < DA_HEAD_DIM, q, zero), jnp.where(lane >= DA_HEAD_DIM, q, zero))
    row = lax.broadcasted_iota(jnp.int32, (tq, tk), 0)
    col = lax.broadcasted_iota(jnp.int32, (tq, tk), 1)
    per_q = tq // tk

    def block(kb, carry, diag_offset):
        start = pl.multiple_of(kb * tk, tk)
        k = k_ref[pl.ds(start, tk), :]
        v = v_ref[pl.ds(start, tk), :]
        new = []
        for mi in range(2):
            m, l, acc = carry[3 * mi:3 * mi + 3]
            s = lax.dot_general(q_maps[mi], k, (((1,), (1,)), ((), ())), preferred_element_type=F32)
            if diag_offset is not None:
                s = jnp.where(col + diag_offset <= row, s, -jnp.inf)
            m_new = jnp.maximum(m, jnp.max(s, axis=1, keepdims=True))
            p = jnp.exp(s - m_new)
            alpha = jnp.exp(m - m_new)
            l = alpha * l + jnp.sum(p, axis=1, keepdims=True)
            acc = alpha * acc + jnp.dot(p.astype(BF16), v, preferred_element_type=F32)
            new += [m_new, l, acc]
        return tuple(new)

    init = (jnp.full((tq, 1), -jnp.inf, F32), jnp.zeros((tq, 1), F32), jnp.zeros((tq, LANES), F32)) * 2
    carry = lax.fori_loop(0, qi * per_q, lambda kb, c: block(kb, c, None), init)
    for d in range(per_q):
        carry = block(qi * per_q + d, carry, d * tk)
    m1, l1, a1, m2, l2, a2 = carry
    o = a1 / l1 - lam_ref[0] * (a2 / l2)
    o_ref[...] = (_rmsnorm_rows(o, g_ref[...]) * (1.0 - lam_init)).astype(o_ref.dtype)


def diff_attention(qk, z, lam, subln_g, *, batch, seq, tq, tk, lam_init):
    T = qk.shape[0]
    nq = seq // tq
    width = DA_HEADS * 2 * DA_HEAD_DIM
    return pl.pallas_call(
        functools.partial(_diff_attn_kernel, tq=tq, tk=tk, lam_init=lam_init),
        grid=(batch, DA_HEADS, nq),
        in_specs=[pl.BlockSpec(memory_space=pltpu.SMEM),
                  pl.BlockSpec((tq, LANES), lambda b, h, i: (b * nq + i, h)),
                  pl.BlockSpec((seq, LANES), lambda b, h, i: (b, DA_HEADS + h)),
                  pl.BlockSpec((seq, LANES), lambda b, h, i: (b, 2 * DA_HEADS + h)),
                  pl.BlockSpec((1, LANES), lambda b, h, i: (0, 0))],
        out_specs=pl.BlockSpec((tq, LANES), lambda b, h, i: (b * nq + i, h)),
        out_shape=jax.ShapeDtypeStruct((T, width), BF16),
        compiler_params=_params("parallel", "parallel", "arbitrary"),
        name="diff_attention",
    )(lam, qk, qk, z, subln_g.reshape(1, LANES))


def _pool_kernel(p_ref, prev_ref, cur_band_ref, prev_band_ref, inv_ref, w_ref, scale_ref, o_ref):
    has_prev = pl.program_id(1) > 0
    for g in range(len(POOL_WINDOWS)):
        cols = slice(g * LANES, (g + 1) * LANES)
        pg = p_ref[:, cols]
        wsum = jnp.dot(cur_band_ref[g], pg, preferred_element_type=F32)
        halo = jnp.dot(prev_band_ref[g], prev_ref[:, cols], preferred_element_type=F32)
        wsum = wsum + jnp.where(has_prev, halo, 0.0)
        d = wsum * inv_ref[:, cols] - pg.astype(F32)
        y = jnp.dot(d.astype(BF16), w_ref[g], preferred_element_type=F32) * scale_ref[:, cols]
        o_ref[:, cols] = y.astype(o_ref.dtype)


def multiscale_pool(z, w_pool, pool_scale, *, batch, seq, tc):
    T = z.shape[0]
    nc = seq // tc
    width = len(POOL_WINDOWS) * LANES
    col_block = 3
    t = jnp.arange(tc)
    delta = t[:, None] - t[None, :]
    win = jnp.asarray(POOL_WINDOWS)[:, None, None]
    cur_band = ((delta >= 0) & (delta < win)).astype(BF16)
    prev_band = ((delta + tc >= 0) & (delta + tc < win)).astype(BF16)
    pos = jnp.arange(seq)
    inv_cnt = 1.0 / jnp.minimum(pos[:, None] + 1, jnp.repeat(jnp.asarray(POOL_WINDOWS), LANES)[None, :]).astype(F32)
    return pl.pallas_call(
        _pool_kernel,
        grid=(batch, nc),
        in_specs=[pl.BlockSpec((tc, width), lambda b, c: (b * nc + c, col_block)),
                  pl.BlockSpec((tc, width), lambda b, c: (b * nc + jnp.maximum(c - 1, 0), col_block)),
                  pl.BlockSpec((len(POOL_WINDOWS), tc, tc), lambda b, c: (0, 0, 0)),
                  pl.BlockSpec((len(POOL_WINDOWS), tc, tc), lambda b, c: (0, 0, 0)),
                  pl.BlockSpec((tc, width), lambda b, c: (c, 0)),
                  pl.BlockSpec((len(POOL_WINDOWS), LANES, LANES), lambda b, c: (0, 0, 0)),
                  pl.BlockSpec((1, width), lambda b, c: (0, 0))],
        out_specs=pl.BlockSpec((tc, width), lambda b, c: (b * nc + c, 0)),
        out_shape=jax.ShapeDtypeStruct((T, width), BF16),
        compiler_params=_params("parallel", "arbitrary"),
        name="multiscale_pool",
    )(z, z, cur_band, prev_band, inv_cnt, w_pool.astype(BF16), pool_scale.reshape(1, width))


def _sgate_kernel(u_ref, v_ref, g_ref, b_ref, ws_ref, bias_ref, o_ref):
    v = v_ref[...].astype(F32)
    mu = jnp.mean(v, axis=-1, keepdims=True)
    vc = v - mu
    var = jnp.mean(vc * vc, axis=-1, keepdims=True)
    vn = (vc * lax.rsqrt(var + EPS) * g_ref[...] + b_ref[...]).astype(BF16)
    for n in range(v.shape[0] // CHUNK):
        rows = slice(n * CHUNK, (n + 1) * CHUNK)
        for g in range(SG_GROUPS):
            cols = slice(g * LANES, (g + 1) * LANES)
            mixed = jnp.dot(ws_ref[g], vn[rows, cols], preferred_element_type=F32) + bias_ref[:, cols]
            o_ref[rows, cols] = (u_ref[rows, cols].astype(F32) * mixed).astype(o_ref.dtype)


def chunk_spatial_gate(z, ln_g, ln_b, w_s, b_s, *, tc):
    T = z.shape[0]
    width = SG_GROUPS * LANES
    ws = (w_s * jnp.tril(jnp.ones((CHUNK, CHUNK), w_s.dtype))).astype(BF16)
    bias = jnp.repeat(b_s.T, LANES, axis=1)
    return pl.pallas_call(
        _sgate_kernel,
        grid=(T // tc,),
        in_specs=[pl.BlockSpec((tc, width), lambda i: (i, 4)),
                  pl.BlockSpec((tc, width), lambda i: (i, 5)),
                  pl.BlockSpec((1, width), lambda i: (0, 0)),
                  pl.BlockSpec((1, width), lambda i: (0, 0)),
                  pl.BlockSpec((SG_GROUPS, CHUNK, CHUNK), lambda i: (0, 0, 0)),
                  pl.BlockSpec((CHUNK, width), lambda i: (0, 0))],
        out_specs=pl.BlockSpec((tc, width), lambda i: (i, 0)),
        out_shape=jax.ShapeDtypeStruct((T, width), BF16),
        compiler_params=_params("parallel"),
        name="chunk_spatial_gate",
    )(z, z, ln_g.reshape(1, width), ln_b.reshape(1, width), ws, bias)


def _merge_kernel(a_ref, p_ref, s_ref, gate_ref, x_ref, wb_ref, wo_ref, o_ref):
    d = x_ref.shape[1]
    merged = None
    for n, br in enumerate((a_ref, p_ref, s_ref)):
        proj = jnp.dot(br[...], wb_ref[n], preferred_element_type=F32)
        gate = jax.nn.sigmoid(gate_ref[:, n * d:(n + 1) * d].astype(F32))
        merged = gate * proj if merged is None else merged + gate * proj
    o_ref[...] = x_ref[...] + jnp.dot(merged.astype(BF16), wo_ref[...], preferred_element_type=F32)


def branch_merge(a_out, p_out, s_out, z, x, w_branch, w_out, *, tm):
    T, D = x.shape
    bw = a_out.shape[1]
    br_spec = pl.BlockSpec((tm, bw), lambda i: (i, 0))
    return pl.pallas_call(
        _merge_kernel,
        grid=(T // tm,),
        in_specs=[br_spec, br_spec, br_spec,
                  pl.BlockSpec((tm, N_BRANCH * D), lambda i: (i, 1)),
                  pl.BlockSpec((tm, D), lambda i: (i, 0)),
                  pl.BlockSpec((N_BRANCH, bw, D), lambda i: (0, 0, 0)),
                  pl.BlockSpec((D, D), lambda i: (0, 0))],
        out_specs=pl.BlockSpec((tm, D), lambda i: (i, 0)),
        out_shape=jax.ShapeDtypeStruct((T, D), F32),
        compiler_params=_params("parallel"),
        name="branch_merge",
    )(a_out, p_out, s_out, z, x, w_branch.astype(BF16), w_out.astype(BF16))


def _cross_kernel(x_ref, g_ref, k_ref, v_ref, wq_ref, wo_ref, o_ref):
    x = x_ref[...]
    d = x.shape[1]
    hd = d // X_HEADS
    xn = _rmsnorm_rows(x, g_ref[...]).astype(BF16)
    q = jnp.dot(xn, wq_ref[...], preferred_element_type=F32).astype(BF16)
    heads = []
    for h in range(X_HEADS):
        cols = slice(h * hd, (h + 1) * hd)
        s = lax.dot_general(q[:, cols], k_ref[:, cols], (((1,), (1,)), ((), ())), preferred_element_type=F32)
        s = s * (hd ** -0.5)
        p = jnp.exp(s - jnp.max(s, axis=1, keepdims=True))
        p = p / jnp.sum(p, axis=1, keepdims=True)
        heads.append(jnp.dot(p.astype(BF16), v_ref[:, cols], preferred_element_type=F32).astype(BF16))
    o = jnp.concatenate(heads, axis=1)
    o_ref[...] = x + jnp.dot(o, wo_ref[...], preferred_element_type=F32)


def cross_attention(x, g, kv, w_q, w_o, *, batch, seq, tm):
    T, D = x.shape
    M = kv.shape[0] // batch
    nb = seq // tm
    return pl.pallas_call(
        _cross_kernel,
        grid=(batch, nb),
        in_specs=[pl.BlockSpec((tm, D), lambda b, i: (b * nb + i, 0)),
                  pl.BlockSpec((1, D), lambda b, i: (0, 0)),
                  pl.BlockSpec((M, D), lambda b, i: (b, 0)),
                  pl.BlockSpec((M, D), lambda b, i: (b, 1)),
                  pl.BlockSpec((D, D), lambda b, i: (0, 0)),
                  pl.BlockSpec((D, D), lambda b, i: (0, 0))],
        out_specs=pl.BlockSpec((tm, D), lambda b, i: (b * nb + i, 0)),
        out_shape=jax.ShapeDtypeStruct((T, D), F32),
        compiler_params=_params("parallel", "parallel"),
        name="cross_attention",
    )(x, g.reshape(1, D), kv, kv, w_q.astype(BF16), w_o.astype(BF16))


def _peer_scores_kernel(x_ref, g_ref, wq_ref, keys_ref, o_ref):
    xn = _rmsnorm_rows(x_ref[...], g_ref[...]).astype(BF16)
    q = jnp.dot(xn, wq_ref[...], preferred_element_type=F32).astype(BF16)
    _, nk, nh, tm = o_ref.shape
    width = keys_ref.shape[2]
    for side in range(2):
        st = lax.dot_general(keys_ref[side], q[:, side * width:(side + 1) * width],
                             (((1,), (1,)), ((), ())), preferred_element_type=F32)
        o_ref[side] = st.reshape(nk, nh, tm)


def peer_scores(x, g, w_pq, sub_keys, *, tm):
    T, D = x.shape
    nh, _, nk, half = sub_keys.shape
    wq = w_pq.reshape(D, nh, 2, half).transpose(0, 2, 1, 3).reshape(D, 2 * nh * half).astype(BF16)
    keys = jnp.einsum('hsnk,hg->snhgk', sub_keys, jnp.eye(nh, dtype=sub_keys.dtype))
    keys = keys.reshape(2, nk * nh, nh * half).astype(BF16)
    return pl.pallas_call(
        _peer_scores_kernel,
        grid=(T // tm,),
        in_specs=[pl.BlockSpec((tm, D), lambda i: (i, 0)),
                  pl.BlockSpec((1, D), lambda i: (0, 0)),
                  pl.BlockSpec((D, 2 * nh * half), lambda i: (0, 0)),
                  pl.BlockSpec((2, nk * nh, nh * half), lambda i: (0, 0, 0))],
        out_specs=pl.BlockSpec((2, nk, nh, tm), lambda i: (0, 0, 0, i)),
        out_shape=jax.ShapeDtypeStruct((2, nk, nh, T), F32),
        compiler_params=_params("parallel"),
        name="peer_scores",
    )(x, g.reshape(1, D), wq, keys)


def _first_argmax(xs):
    nodes = [(xs[i], float(i)) for i in range(len(xs))]
    while len(nodes) > 1:
        merged = []
        for j in range(0, len(nodes) - 1, 2):
            (va, ia), (vb, ib) = nodes[j], nodes[j + 1]
            merged.append((jnp.maximum(va, vb), jnp.where(va >= vb, ia, ib)))
        if len(nodes) % 2:
            merged.append(nodes[-1])
        nodes = merged
    return nodes[0]


def _peer_route_kernel(s_ref, rank_ref, e2_ref, cnt_ref, c1_ref, work_ref, top_ref, arg_ref):
    nk = s_ref.shape[1]
    shape = s_ref.shape[1:]
    key_iota = lax.broadcasted_iota(jnp.int32, shape, 0).astype(F32)

    for side in range(2):
        work_ref[...] = s_ref[side]

        def extract(r, _, side=side):
            s = work_ref[...]
            m, idx = _first_argmax(s)
            work_ref[...] = jnp.where(key_iota == idx[None], -jnp.inf, s)
            top_ref[side, r] = m
            arg_ref[side, r] = idx
            return 0

        lax.fori_loop(0, PEER_TOPK, extract, 0)

    t0 = [top_ref[0, a] for a in range(PEER_TOPK)]
    t1 = [top_ref[1, b] for b in range(PEER_TOPK)]
    cand = [t0[a] + t1[b] for a, b in PEER_CELLS]
    best = cand[0]
    live = list(cand)
    chosen = [None] * len(PEER_CELLS)
    for _ in range(PEER_TOPK):
        _, first = _first_argmax(live)
        for ci in range(len(PEER_CELLS)):
            hit = first == float(ci)
            chosen[ci] = hit if chosen[ci] is None else chosen[ci] | hit
            live[ci] = jnp.where(hit, -jnp.inf, live[ci])
    denom = None
    cnt_a = [jnp.zeros(best.shape, F32) for _ in range(PEER_TOPK)]
    for ci, (a, b) in enumerate(PEER_CELLS):
        term = jnp.where(chosen[ci], jnp.exp(cand[ci] - best), 0.0)
        denom = term if denom is None else denom + term
        cnt_a[a] = cnt_a[a] + jnp.where(chosen[ci], 1.0, 0.0)
    inv_denom = 1.0 / denom

    cnt = jnp.zeros(shape, F32)
    rank = jnp.full(shape, float(PEER_TOPK), F32)
    for r in range(PEER_TOPK):
        cnt = jnp.where(key_iota == arg_ref[0, r][None], cnt_a[r][None], cnt)
        rank = jnp.where(key_iota == arg_ref[1, r][None], float(r), rank)
    cnt_ref[:, 0] = cnt
    c1_ref[:, 0] = jnp.exp(s_ref[0] - t0[0][None]) * (0.5 * inv_denom)[None]
    rank_ref[...] = rank
    e2_ref[...] = jnp.exp(s_ref[1] - t1[0][None])


def peer_route(scores):
    _, nk, nh, T = scores.shape
    tb = LANES
    spec = pl.BlockSpec((nk, nh, tb), lambda i: (0, 0, i))
    out = jax.ShapeDtypeStruct((nk, nh, T), F32)
    row_spec = pl.BlockSpec((nk, 1, nh, tb), lambda i: (0, i, 0, 0))
    row_out = jax.ShapeDtypeStruct((nk, T // tb, nh, tb), F32)
    return pl.pallas_call(
        _peer_route_kernel,
        grid=(T // tb,),
        in_specs=[pl.BlockSpec((2, nk, nh, tb), lambda i: (0, 0, 0, i))],
        out_specs=[spec, spec, row_spec, row_spec],
        out_shape=[out, out, row_out, row_out],
        scratch_shapes=[pltpu.VMEM((nk, nh, tb), F32),
                        pltpu.VMEM((2, PEER_TOPK, nh, tb), F32),
                        pltpu.VMEM((2, PEER_TOPK, nh, tb), F32)],
        compiler_params=_params("parallel"),
        name="peer_route",
    )(scores)


def _row_tile(ref, tile, hd, rows):
    parts = [ref[tile, q, pl.ds(hd, rows, stride=0), :] for q in range(ref.shape[1])]
    return jnp.concatenate(parts, axis=-1).astype(BF16)


def _peer_dense_kernel(x_ref, g_ref, gout_ref, u_ref, vt_ref, rank_ref, e2_ref, cnt_ref, c1_ref, o_ref, xnt_ref,
                       acc_ref, *, chunk, norm_out):
    e = pl.program_id(1)

    @pl.when(e == 0)
    def _():
        xn = _rmsnorm_rows(x_ref[...], g_ref[...])
        xnt_ref[...] = xn.T.astype(BF16)
        acc_ref[...] = jnp.zeros_like(acc_ref)

    nh, groups, rows, tb = rank_ref.shape
    nk = groups * rows
    zero = jnp.zeros((groups, rows, tb), BF16)
    xnt = xnt_ref[...]
    total = None
    for c in range(u_ref.shape[0] // chunk):
        h = jnp.dot(u_ref[c * chunk:(c + 1) * chunk, :], xnt, preferred_element_type=F32)
        hb = h.astype(BF16)
        act = hb * (1.0 + lax.erf(hb * (2.0 ** -0.5)))
        n_tiles = chunk // nk
        act = act.reshape(n_tiles, groups, rows, tb)
        w = [None] * n_tiles
        for hd in range(nh):
            rank = rank_ref[hd]
            e2 = e2_ref[hd]
            for ii in range(n_tiles):
                tile = c * n_tiles + ii
                cnt = _row_tile(cnt_ref, tile, hd, rows)
                c1 = _row_tile(c1_ref, tile, hd, rows)
                term = jnp.where(rank < cnt[None], e2 * c1[None], zero)
                w[ii] = term if w[ii] is None else w[ii] + term
        a = jnp.stack([act[ii] * w[ii] for ii in range(n_tiles)], axis=0).reshape(chunk, tb)
        part = jnp.dot(vt_ref[:, c * chunk:(c + 1) * chunk], a, preferred_element_type=F32)
        total = part if total is None else total + part
    acc_ref[...] += total

    @pl.when(e == pl.num_programs(1) - 1)
    def _():
        y = x_ref[...] + acc_ref[...].T
        o_ref[...] = _rmsnorm_rows(y, gout_ref[...]) if norm_out else y


def peer_dense(x, g, g_out, u, vt, rank, e2, cnt, c1, *, tb, te, chunk, norm_out):
    T, D = x.shape
    E = u.shape[0]
    nh, nk, _ = rank.shape
    rank = rank.reshape(nh, nk // BF16_ROWS, BF16_ROWS, T)
    e2 = e2.reshape(nh, nk // BF16_ROWS, BF16_ROWS, T)
    tab_spec = pl.BlockSpec((nh, nk // BF16_ROWS, BF16_ROWS, tb), lambda i, e: (0, 0, 0, i))
    row_spec = pl.BlockSpec((te // nk, tb // LANES, nh, LANES), lambda i, e: (e, i, 0, 0))
    return pl.pallas_call(
        functools.partial(_peer_dense_kernel, chunk=chunk, norm_out=norm_out),
        grid=(T // tb, E // te),
        in_specs=[pl.BlockSpec((tb, D), lambda i, e: (i, 0)),
                  pl.BlockSpec((1, D), lambda i, e: (0, 0)),
                  pl.BlockSpec((1, D), lambda i, e: (0, 0)),
                  pl.BlockSpec((te, D), lambda i, e: (e, 0)),
                  pl.BlockSpec((D, te), lambda i, e: (0, e)),
                  tab_spec, tab_spec, row_spec, row_spec],
        out_specs=pl.BlockSpec((tb, D), lambda i, e: (i, 0)),
        out_shape=jax.ShapeDtypeStruct((T, D), F32),
        scratch_shapes=[pltpu.VMEM((D, tb), BF16), pltpu.VMEM((D, tb), F32)],
        compiler_params=_params("parallel", "arbitrary"),
        name="peer_dense",
    )(x, g.reshape(1, D), g_out.reshape(1, D), u, vt, rank, e2, cnt, c1)


def _expert_tables_kernel(u_ref, v_ref, ub_ref, vt_ref):
    ub_ref[...] = u_ref[0].astype(ub_ref.dtype)
    vt_ref[...] = v_ref[0].T.astype(vt_ref.dtype)


def expert_tables(peer_u, peer_v, layer, *, te):
    _, E, D = peer_u.shape
    in_spec = pl.BlockSpec((1, te, D), lambda e: (layer, e, 0))
    return pl.pallas_call(
        _expert_tables_kernel,
        grid=(E // te,),
        in_specs=[in_spec, in_spec],
        out_specs=[pl.BlockSpec((te, D), lambda e: (e, 0)), pl.BlockSpec((D, te), lambda e: (0, e))],
        out_shape=[jax.ShapeDtypeStruct((E, D), BF16), jax.ShapeDtypeStruct((D, E), BF16)],
        compiler_params=_params("parallel"),
        name="expert_tables",
    )(peer_u, peer_v)


def _tile(n, want):
    if n <= want:
        return n
    t = want - want % LANES
    while n % t:
        t -= LANES
    return t


def kernel(x, mem, positions, g_mix, w_in, lam_qk, subln_g, w_pool, pool_scale, sg_ln_g, sg_ln_b, w_spatial,
           b_spatial, w_branch, w_out, g_cross, g_mem, w_xq, w_xkv, w_xo, g_ffn, w_pq, sub_keys, peer_u, peer_v,
           g_final):
    B, S, D = x.shape
    T = B * S
    depth = w_in.shape[0]
    x = x.reshape(T, D)
    memf = mem.reshape(-1, D)
    cos_tab, sin_tab, perm = rope_tables(positions)

    tm = _tile(S, 1024)
    tb_dense = _tile(S, 1024)

    for l in range(depth):
        lam_init = 0.8 - 0.6 * math.exp(-0.3 * l)
        lq = lam_qk[l].astype(F32)
        lam = (jnp.exp(jnp.sum(lq[0] * lq[1])) - jnp.exp(jnp.sum(lq[2] * lq[3])) + lam_init).reshape(1)

        z = norm_matmul(x, g_mix[l], w_in[l].astype(BF16), tm=tm, tn=1536, name="in_proj")
        qk = rope_qk(z, cos_tab, sin_tab, perm, tm=tm)
        a_out = diff_attention(qk, z, lam, subln_g[l], batch=B, seq=S, tq=_tile(S, 1024), tk=_tile(S, 512),
                               lam_init=lam_init)
        p_out = multiscale_pool(z, w_pool[l], pool_scale[l], batch=B, seq=S, tc=_tile(S, 256))
        s_out = chunk_spatial_gate(z, sg_ln_g[l], sg_ln_b[l], w_spatial[l], b_spatial[l], tc=_tile(S, 512))
        x = branch_merge(a_out, p_out, s_out, z, x, w_branch[l], w_out[l], tm=_tile(S, 512))

        kv = norm_matmul(memf, g_mem[l], w_xkv[l].astype(BF16), tm=_tile(memf.shape[0], 1024), tn=1024,
                         name="mem_kv")
        x = cross_attention(x, g_cross[l], kv, w_xq[l], w_xo[l], batch=B, seq=S, tm=_tile(S, 512))

        scores = peer_scores(x, g_ffn[l], w_pq[l], sub_keys[l], tm=_tile(S, 512))
        rank, e2, cnt, c1 = peer_route(scores)
        u_bf, vt_bf = expert_tables(peer_u, peer_v, l, te=512)
        x = peer_dense(x, g_ffn[l], g_final, u_bf, vt_bf,
                       rank.transpose(1, 0, 2).astype(BF16), e2.transpose(1, 0, 2).astype(BF16), cnt, c1,
                       tb=tb_dense, te=1024, chunk=1024, norm_out=(l == depth - 1))

    return x.reshape(B, S, D)
```
